```python
import math
import jax, jax.numpy as jnp
from jax import lax
import numpy as np

D_MODEL = 1024
BATCH = 16
SEQ = 2048
DEPTH = 2
DEC_BATCH = 4
DEC_SEQ = 8192
PAST_LEN = 128

N_EVEN = (DEPTH + 1) // 2
N_ODD = DEPTH // 2
HEAD_DIM = 64
EPS = 1e-6
ROPE_THETA = 10000.0
NEG_INF = -1e30
A_GROUPS = ((128, 1), (512, 4), (2048, 16))
A_HEADS_PER_GROUP = 4
A_HEADS = A_HEADS_PER_GROUP * len(A_GROUPS)
A_WIDTH = A_HEADS * HEAD_DIM
A_OUT = A_HEADS_PER_GROUP * HEAD_DIM
B_WIDTH = 256
B_GROUP_CH = 16
B_GROUPS = B_WIDTH // B_GROUP_CH
B_STATE = 64
DT_MIN = 0.001
DT_MAX = 0.1
MIX_E_IN = 3 * A_WIDTH + B_WIDTH
MIX_E_OUT = A_OUT + B_WIDTH
GRID_W = 64
C_Q_HEADS = 16
C_KV_HEADS = 4
C_Q_WIDTH = C_Q_HEADS * HEAD_DIM
C_KV_WIDTH = C_KV_HEADS * HEAD_DIM
C_Q_BLOCK = 128
MIX_O_IN = C_Q_WIDTH + 2 * C_KV_WIDTH
D_FF = 2816
N_EXPERTS = 8
TOP_K = 2
D_EXPERT = 3584
PLE_DIM = 256

kernel_name = 'hybrid_dilated_s5_axial_gqa_encoder'


def rms_norm(x, g):
    xf = x.astype(jnp.float32)
    y = xf * lax.rsqrt(jnp.mean(xf * xf, axis=-1, keepdims=True) + EPS)
    return (y * g.astype(jnp.float32)).astype(x.dtype)


def rope_cos_sin(pos, dim):
    inv = ROPE_THETA ** (-jnp.arange(0, dim, 2, dtype=jnp.float32) / dim)
    ang = pos.astype(jnp.float32)[:, None] * inv[None, :]
    ang = jnp.concatenate([ang, ang], axis=-1)
    return jnp.cos(ang), jnp.sin(ang)


def apply_rope(x, cos, sin):
    x1, x2 = jnp.split(x, 2, axis=-1)
    rot = jnp.concatenate([-x2, x1], axis=-1)
    out = x.astype(jnp.float32) * cos[:, None, :] + rot.astype(jnp.float32) * sin[:, None, :]
    return out.astype(x.dtype)


def axial_rope(x, n_rows):
    row = jnp.repeat(jnp.arange(n_rows), GRID_W)
    col = jnp.tile(jnp.arange(GRID_W), n_rows)
    half = x.shape[-1] // 2
    cr, sr = rope_cos_sin(row, half)
    cc, sc = rope_cos_sin(col, half)
    return jnp.concatenate([apply_rope(x[..., :half], cr, sr), apply_rope(x[..., half:], cc, sc)], axis=-1)


def dilated_attention_group(q, k, v, window, dilation):
    bsz, s_len, n_h, dh = q.shape
    half = window // (2 * dilation)
    sub_len = s_len // dilation
    n_blk = -(-sub_len // half)
    pad_len = n_blk * half

    def to_sub(t):
        t = t.reshape(bsz, sub_len, dilation, n_h, dh).transpose(0, 2, 1, 3, 4)
        return jnp.pad(t, ((0, 0), (0, 0), (0, pad_len - sub_len), (0, 0), (0, 0)))

    def banded(t):
        t = jnp.pad(t, ((0, 0), (0, 0), (half, half), (0, 0), (0, 0)))
        t = t.reshape(bsz, dilation, n_blk + 2, half, n_h, dh)
        return jnp.concatenate([t[:, :, :-2], t[:, :, 1:-1], t[:, :, 2:]], axis=3)

    qb = to_sub(q).reshape(bsz, dilation, n_blk, half, n_h, dh)
    kb = banded(to_sub(k))
    vb = banded(to_sub(v))
    s = jnp.einsum('brnqhd,brnkhd->brnhqk', qb, kb, preferred_element_type=jnp.float32) * (dh ** -0.5)
    q_idx = jnp.arange(half)[:, None]
    k_idx = jnp.arange(3 * half)[None, :]
    k_pos = jnp.arange(n_blk)[:, None, None] * half + k_idx - half
    valid = (jnp.abs(k_idx - half - q_idx) <= half) & (k_pos >= 0) & (k_pos < sub_len)
    s = jnp.where(valid[None, None, :, None], s, NEG_INF)
    m = jnp.max(s, axis=-1, keepdims=True)
    e = jnp.exp(s - m)
    den = jnp.sum(e, axis=-1, keepdims=True)
    o = jnp.einsum('brnhqk,brnkhd->brnqhd', (e / den).astype(v.dtype), vb)
    lse = (m + jnp.log(den))[..., 0]
    o = o.reshape(bsz, dilation, pad_len, n_h, dh)[:, :, :sub_len]
    o = o.transpose(0, 2, 1, 3, 4).reshape(bsz, s_len, n_h, dh)
    lse = lse.transpose(0, 1, 2, 4, 3).reshape(bsz, dilation, pad_len, n_h)[:, :, :sub_len]
    lse = lse.transpose(0, 2, 1, 3).reshape(bsz, s_len, n_h)
    return o, lse


def _complex_linear_combine(c1, c2):
    a1r, a1i, b1r, b1i = c1
    a2r, a2i, b2r, b2i = c2
    return (a2r * a1r - a2i * a1i,
            a2r * a1i + a2i * a1r,
            a2r * b1r - a2i * b1i + b2r,
            a2r * b1i + a2i * b1r + b2i)


def s5_mixer(u, lam_re, lam_im, log_dt, b_re, b_im, c_re, c_im, d_skip, w_glu):
    f32 = jnp.float32
    bsz, s_len, _ = u.shape
    ug = u.astype(f32).reshape(bsz, s_len, B_GROUPS, B_GROUP_CH)
    y = ug * d_skip.astype(f32).reshape(B_GROUPS, B_GROUP_CH)
    for direction in range(2):
        lr = lam_re[direction].astype(f32)
        li = lam_im[direction].astype(f32)
        dt = jnp.exp(log_dt[direction].astype(f32))[:, None]
        mag = jnp.exp(lr * dt)
        a_r = mag * jnp.cos(li * dt)
        a_i = mag * jnp.sin(li * dt)
        den = lr * lr + li * li
        f_r = ((a_r - 1.0) * lr + a_i * li) / den
        f_i = (a_i * lr - (a_r - 1.0) * li) / den
        br = b_re[direction].astype(f32)
        bi = b_im[direction].astype(f32)
        bb_r = f_r[..., None] * br - f_i[..., None] * bi
        bb_i = f_r[..., None] * bi + f_i[..., None] * br
        bu_r = jnp.einsum('bsgc,gpc->bsgp', ug, bb_r)
        bu_i = jnp.einsum('bsgc,gpc->bsgp', ug, bb_i)
        a_r_full = jnp.broadcast_to(a_r, bu_r.shape)
        a_i_full = jnp.broadcast_to(a_i, bu_r.shape)
        _, _, h_r, h_i = lax.associative_scan(_complex_linear_combine, (a_r_full, a_i_full, bu_r, bu_i),
                                              reverse=(direction == 1), axis=1)
        y = y + jnp.einsum('gcp,bsgp->bsgc', c_re[direction].astype(f32), h_r) \
              - jnp.einsum('gcp,bsgp->bsgc', c_im[direction].astype(f32), h_i)
    z = jax.nn.gelu(y.reshape(bsz, s_len, B_WIDTH))
    z = z * jax.nn.sigmoid(z @ w_glu.astype(f32))
    return z.astype(u.dtype)


def mixer_even(x, ln_mix, w_in, a_qnorm, a_knorm, lam_re, lam_im, log_dt, b_re, b_im, c_re, c_im,
               d_skip, w_glu, w_out):
    bsz, s_len, _ = x.shape
    h = rms_norm(x, ln_mix)
    proj = h @ w_in
    q, k, v, u = jnp.split(proj, [A_WIDTH, 2 * A_WIDTH, 3 * A_WIDTH], axis=-1)
    shp = (bsz, s_len, A_HEADS, HEAD_DIM)
    cos, sin = rope_cos_sin(jnp.arange(s_len), HEAD_DIM)
    q = apply_rope(rms_norm(q.reshape(shp), a_qnorm), cos, sin)
    k = apply_rope(rms_norm(k.reshape(shp), a_knorm), cos, sin)
    v = v.reshape(shp)
    outs, lses = [], []
    for g, (window, dilation) in enumerate(A_GROUPS):
        hs = slice(g * A_HEADS_PER_GROUP, (g + 1) * A_HEADS_PER_GROUP)
        o, l = dilated_attention_group(q[:, :, hs], k[:, :, hs], v[:, :, hs], window, dilation)
        outs.append(o)
        lses.append(l)
    mix_w = jax.nn.softmax(jnp.stack(lses, axis=0), axis=0)
    a_out = jnp.sum(mix_w[..., None] * jnp.stack(outs, axis=0).astype(jnp.float32), axis=0)
    a_out = a_out.reshape(bsz, s_len, A_OUT).astype(x.dtype)
    b_out = s5_mixer(u, lam_re, lam_im, log_dt, b_re, b_im, c_re, c_im, d_skip, w_glu)
    return x + jnp.concatenate([a_out, b_out], axis=-1) @ w_out


def block_gqa_attention(q, k, v):
    bsz, s_len, n_q, dh = q.shape
    rep = n_q // C_KV_HEADS
    n_blk = s_len // C_Q_BLOCK
    qb = q.reshape(bsz, n_blk, C_Q_BLOCK, C_KV_HEADS, rep, dh).transpose(1, 0, 2, 3, 4, 5)
    scale = dh ** -0.5

    def one_block(q_blk):
        s = jnp.einsum('bqgrd,bkgd->bgrqk', q_blk, k, preferred_element_type=jnp.float32) * scale
        p = jax.nn.softmax(s, axis=-1).astype(v.dtype)
        return jnp.einsum('bgrqk,bkgd->bqgrd', p, v)

    o = lax.map(one_block, qb)
    return o.transpose(1, 0, 2, 3, 4, 5).reshape(bsz, s_len, n_q * dh)


def mixer_odd(x, ln_mix, w_in, c_qnorm, c_knorm, w_out):
    bsz, s_len, _ = x.shape
    n_rows = s_len // GRID_W
    h = rms_norm(x, ln_mix)
    q, k, v = jnp.split(h @ w_in, [C_Q_WIDTH, C_Q_WIDTH + C_KV_WIDTH], axis=-1)
    q = axial_rope(rms_norm(q.reshape(bsz, s_len, C_Q_HEADS, HEAD_DIM), c_qnorm), n_rows)
    k = axial_rope(rms_norm(k.reshape(bsz, s_len, C_KV_HEADS, HEAD_DIM), c_knorm), n_rows)
    v = v.reshape(bsz, s_len, C_KV_HEADS, HEAD_DIM)
    return x + block_gqa_attention(q, k, v) @ w_out


def swiglu(h, w_gate, w_up, w_down):
    return (jax.nn.silu(h @ w_gate) * (h @ w_up)) @ w_down


def moe_swiglu(h, router_w, w_gate, w_up, w_down):
    bsz, s_len, dm = h.shape
    hf = h.reshape(-1, dm)
    logits = jnp.matmul(hf, router_w, preferred_element_type=jnp.float32)
    top_val, top_idx = lax.top_k(logits, TOP_K)
    top_w = jax.nn.softmax(top_val, axis=-1)
    gates = jnp.einsum('nk,nke->ne', top_w, jax.nn.one_hot(top_idx, N_EXPERTS, dtype=jnp.float32))
    out = jnp.zeros(hf.shape, jnp.float32)
    for e in range(N_EXPERTS):
        out = out + gates[:, e:e + 1] * swiglu(hf, w_gate[e], w_up[e], w_down[e]).astype(jnp.float32)
    return out.astype(h.dtype).reshape(bsz, s_len, dm)


def per_layer_embed(x, p, ln, w_proj, w_gate):
    gate = jax.nn.sigmoid(rms_norm(x, ln) @ w_gate)
    return x + (p @ w_proj) * gate


def trunk(x, p, ln_mix_e, w_in_e, a_qnorm, a_knorm, ssm_lam_re, ssm_lam_im, ssm_log_dt, ssm_b_re, ssm_b_im,
          ssm_c_re, ssm_c_im, ssm_d, ssm_w_glu, w_out_e, ln_ffn_e, ffn_w_gate, ffn_w_up, ffn_w_down,
          ln_mix_o, w_in_o, c_qnorm, c_knorm, w_out_o, ln_ffn_o, router_w, moe_w_gate, moe_w_up, moe_w_down,
          ple_ln, ple_w_proj, ple_w_gate):
    for i in range(DEPTH):
        j = i // 2
        if i % 2 == 0:
            x = mixer_even(x, ln_mix_e[j], w_in_e[j], a_qnorm[j], a_knorm[j], ssm_lam_re[j], ssm_lam_im[j],
                           ssm_log_dt[j], ssm_b_re[j], ssm_b_im[j], ssm_c_re[j], ssm_c_im[j], ssm_d[j],
                           ssm_w_glu[j], w_out_e[j])
            x = x + swiglu(rms_norm(x, ln_ffn_e[j]), ffn_w_gate[j], ffn_w_up[j], ffn_w_down[j])
        else:
            x = mixer_odd(x, ln_mix_o[j], w_in_o[j], c_qnorm[j], c_knorm[j], w_out_o[j])
            x = x + moe_swiglu(rms_norm(x, ln_ffn_o[j]), router_w[j], moe_w_gate[j], moe_w_up[j], moe_w_down[j])
        x = per_layer_embed(x, p[i], ple_ln[i], ple_w_proj[i], ple_w_gate[i])
    return x


def setup_inputs(seed: int = 0) -> dict:
    key = jax.random.key(seed)
    ks = iter(jax.random.split(key, 40))
    f32 = jnp.float32

    def nrm(shape, scale):
        return jax.random.normal(next(ks), shape, f32) * scale

    def gain(shape):
        return 1.0 + nrm(shape, 0.02)

    d = D_MODEL
    inputs = {}
    inputs['x_prompt'] = nrm((BATCH, SEQ, d), 1.0)
    inputs['x_sample'] = nrm((DEC_BATCH, DEC_SEQ, d), 1.0)
    inputs['p_prompt'] = nrm((DEPTH, BATCH, SEQ, PLE_DIM), 1.0)
    inputs['p_sample'] = nrm((DEPTH, DEC_BATCH, DEC_SEQ, PLE_DIM), 1.0)
    inputs['ln_mix_e'] = gain((N_EVEN, d))
    inputs['w_in_e'] = nrm((N_EVEN, d, MIX_E_IN), d ** -0.5)
    inputs['a_qnorm'] = gain((N_EVEN, HEAD_DIM))
    inputs['a_knorm'] = gain((N_EVEN, HEAD_DIM))
    inputs['ssm_lam_re'] = -0.5 + nrm((N_EVEN, 2, B_GROUPS, B_STATE), 0.01)
    inputs['ssm_lam_im'] = jnp.pi * jnp.arange(B_STATE, dtype=f32) + nrm((N_EVEN, 2, B_GROUPS, B_STATE), 0.01)
    inputs['ssm_log_dt'] = jax.random.uniform(next(ks), (N_EVEN, 2, B_GROUPS), f32,
                                              minval=math.log(DT_MIN), maxval=math.log(DT_MAX))
    inputs['ssm_b_re'] = nrm((N_EVEN, 2, B_GROUPS, B_STATE, B_GROUP_CH), (2 * B_GROUP_CH) ** -0.5)
    inputs['ssm_b_im'] = nrm((N_EVEN, 2, B_GROUPS, B_STATE, B_GROUP_CH), (2 * B_GROUP_CH) ** -0.5)
    inputs['ssm_c_re'] = nrm((N_EVEN, 2, B_GROUPS, B_GROUP_CH, B_STATE), B_STATE ** -0.5)
    inputs['ssm_c_im'] = nrm((N_EVEN, 2, B_GROUPS, B_GROUP_CH, B_STATE), B_STATE ** -0.5)
    inputs['ssm_d'] = nrm((N_EVEN, B_WIDTH), 1.0)
    inputs['ssm_w_glu'] = nrm((N_EVEN, B_WIDTH, B_WIDTH), B_WIDTH ** -0.5)
    inputs['w_out_e'] = nrm((N_EVEN, MIX_E_OUT, d), MIX_E_OUT ** -0.5)
    inputs['ln_ffn_e'] = gain((N_EVEN, d))
    inputs['ffn_w_gate'] = nrm((N_EVEN, d, D_FF), d ** -0.5)
    inputs['ffn_w_up'] = nrm((N_EVEN, d, D_FF), d ** -0.5)
    inputs['ffn_w_down'] = nrm((N_EVEN, D_FF, d), D_FF ** -0.5)
    inputs['ln_mix_o'] = gain((N_ODD, d))
    inputs['w_in_o'] = nrm((N_ODD, d, MIX_O_IN), d ** -0.5)
    inputs['c_qnorm'] = gain((N_ODD, HEAD_DIM))
    inputs['c_knorm'] = gain((N_ODD, HEAD_DIM))
    inputs['w_out_o'] = nrm((N_ODD, C_Q_WIDTH, d), C_Q_WIDTH ** -0.5)
    inputs['ln_ffn_o'] = gain((N_ODD, d))
    inputs['router_w'] = nrm((N_ODD, d, N_EXPERTS), d ** -0.5)
    inputs['moe_w_gate'] = nrm((N_ODD, N_EXPERTS, d, D_EXPERT), d ** -0.5)
    inputs['moe_w_up'] = nrm((N_ODD, N_EXPERTS, d, D_EXPERT), d ** -0.5)
    inputs['moe_w_down'] = nrm((N_ODD, N_EXPERTS, D_EXPERT, d), D_EXPERT ** -0.5)
    inputs['ple_ln'] = gain((DEPTH, d))
    inputs['ple_w_proj'] = nrm((DEPTH, PLE_DIM, d), PLE_DIM ** -0.5)
    inputs['ple_w_gate'] = nrm((DEPTH, d, d), d ** -0.5)
    return inputs


def reference(x_prompt, x_sample, p_prompt, p_sample, ln_mix_e, w_in_e, a_qnorm, a_knorm, ssm_lam_re,
              ssm_lam_im, ssm_log_dt, ssm_b_re, ssm_b_im, ssm_c_re, ssm_c_im, ssm_d, ssm_w_glu, w_out_e,
              ln_ffn_e, ffn_w_gate, ffn_w_up, ffn_w_down, ln_mix_o, w_in_o, c_qnorm, c_knorm, w_out_o,
              ln_ffn_o, router_w, moe_w_gate, moe_w_up, moe_w_down, ple_ln, ple_w_proj, ple_w_gate):
    weights = (ln_mix_e, w_in_e, a_qnorm, a_knorm, ssm_lam_re, ssm_lam_im, ssm_log_dt, ssm_b_re, ssm_b_im,
               ssm_c_re, ssm_c_im, ssm_d, ssm_w_glu, w_out_e, ln_ffn_e, ffn_w_gate, ffn_w_up, ffn_w_down,
               ln_mix_o, w_in_o, c_qnorm, c_knorm, w_out_o, ln_ffn_o, router_w, moe_w_gate, moe_w_up,
               moe_w_down, ple_ln, ple_w_proj, ple_w_gate)
    y_prompt = trunk(x_prompt, p_prompt, *weights)
    y_sample = trunk(x_sample, p_sample, *weights)
    return (y_prompt, y_sample)
```

```python
import functools
import math

import jax
import jax.numpy as jnp
import numpy as np
from jax import lax
from jax.experimental import pallas as pl
from jax.experimental.pallas import tpu as pltpu

F32 = jnp.float32
BF16 = jnp.bfloat16

D_MODEL = 1024
HEAD_DIM = 64
EPS = 1e-6
ROPE_THETA = 10000.0
NEG_INF = -1e30
LANES = 128
HEAD_BLOCK = 256
HEADS_PER_BLOCK = HEAD_BLOCK // HEAD_DIM
A_GROUPS = ((128, 1), (512, 4), (2048, 16))
A_HALF = 64
A_WIDTH = 768
B_WIDTH = 256
B_GROUP_CH = 16
B_GROUPS = 16
B_STATE = 64
SSM_N = B_GROUPS * B_STATE
QKV_E = 3 * A_WIDTH
D_FF = 2816
GRID_W = 64
C_Q_WIDTH = 1024
C_KV_WIDTH = 256
N_EXPERTS = 8
D_EXPERT = 3584
PLE_DIM = 256

VMEM_LIMIT = 56 * 1024 * 1024


def _cparams(sem):
    return pltpu.CompilerParams(dimension_semantics=sem, vmem_limit_bytes=VMEM_LIMIT)


def _rms(x, g):
    ms = jnp.mean(x * x, axis=-1, keepdims=True)
    return x * lax.rsqrt(ms + EPS) * g


def _headnorm_rope(y, bd, gain, cos, sin_signed, half):
    ss = jnp.dot((y * y).astype(BF16), bd, preferred_element_type=F32)
    yn = y * lax.rsqrt(ss + EPS) * gain
    width = yn.shape[1]
    lane = lax.broadcasted_iota(jnp.int32, yn.shape, 1)
    up = pltpu.roll(yn, width - half, 1)
    dn = pltpu.roll(yn, half, 1)
    rot = jnp.where((lane & (2 * half - 1)) < half, up, dn)
    return yn * cos + rot * sin_signed


def _proj_even_body(x_ref, g_ref, w_ref, bd_ref, qn_ref, kn_ref, cos_ref, sin_ref, qkv_ref, u_ref):
    xn = _rms(x_ref[...], g_ref[...]).astype(BF16)
    n_head_blocks = A_WIDTH // HEAD_BLOCK
    for c in range(QKV_E // HEAD_BLOCK + 1):
        y = jnp.dot(xn, w_ref[:, c * HEAD_BLOCK:(c + 1) * HEAD_BLOCK], preferred_element_type=F32)
        if c < 2 * n_head_blocks:
            gain = qn_ref[...] if c < n_head_blocks else kn_ref[...]
            y = _headnorm_rope(y, bd_ref[...], gain, cos_ref[...], sin_ref[...], HEAD_DIM // 2)
            if c < n_head_blocks:
                y = y * (HEAD_DIM ** -0.5)
        if c < QKV_E // HEAD_BLOCK:
            qkv_ref[:, c * HEAD_BLOCK:(c + 1) * HEAD_BLOCK] = y.astype(BF16)
        else:
            u_ref[...] = y


def _proj_even(x, ln, w, bd, qn, kn, cos, sin, seq, tm):
    n = x.shape[0]
    n_seq_tiles = seq // tm
    const = lambda i: (0, 0)
    return pl.pallas_call(
        _proj_even_body,
        out_shape=(jax.ShapeDtypeStruct((n, QKV_E), BF16), jax.ShapeDtypeStruct((n, B_WIDTH), F32)),
        grid=(n // tm,),
        in_specs=[
            pl.BlockSpec((tm, D_MODEL), lambda i: (i, 0)),
            pl.BlockSpec((1, D_MODEL), const),
            pl.BlockSpec(w.shape, const),
            pl.BlockSpec(bd.shape, const),
            pl.BlockSpec((1, HEAD_BLOCK), const),
            pl.BlockSpec((1, HEAD_BLOCK), const),
            pl.BlockSpec((tm, HEAD_BLOCK), lambda i: (i % n_seq_tiles, 0)),
            pl.BlockSpec((tm, HEAD_BLOCK), lambda i: (i % n_seq_tiles, 0)),
        ],
        out_specs=(pl.BlockSpec((tm, QKV_E), lambda i: (i, 0)), pl.BlockSpec((tm, B_WIDTH), lambda i: (i, 0))),
        compiler_params=_cparams(("arbitrary",)),
        name="proj_even",
    )(x, ln, w, bd, qn, kn, cos, sin)


def _proj_odd_body(x_ref, g_ref, w_ref, bd_ref, qn_ref, kn_ref, cos_ref, sin_ref, q_ref, kv_ref):
    xn = _rms(x_ref[...], g_ref[...]).astype(BF16)
    n_q_blocks = C_Q_WIDTH // HEAD_BLOCK
    for c in range(n_q_blocks + 2):
        y = jnp.dot(xn, w_ref[:, c * HEAD_BLOCK:(c + 1) * HEAD_BLOCK], preferred_element_type=F32)
        if c <= n_q_blocks:
            gain = qn_ref[...] if c < n_q_blocks else kn_ref[...]
            y = _headnorm_rope(y, bd_ref[...], gain, cos_ref[...], sin_ref[...], HEAD_DIM // 4)
        if c < n_q_blocks:
            q_ref[:, c * HEAD_BLOCK:(c + 1) * HEAD_BLOCK] = (y * (HEAD_DIM ** -0.5)).astype(BF16)
        else:
            kv_ref[:, (c - n_q_blocks) * HEAD_BLOCK:(c - n_q_blocks + 1) * HEAD_BLOCK] = y.astype(BF16)


def _proj_odd(x, ln, w, bd, qn, kn, cos, sin, seq, tm):
    n = x.shape[0]
    n_seq_tiles = seq // tm
    const = lambda i: (0, 0)
    return pl.pallas_call(
        _proj_odd_body,
        out_shape=(jax.ShapeDtypeStruct((n, C_Q_WIDTH), BF16), jax.ShapeDtypeStruct((n, 2 * C_KV_WIDTH), BF16)),
        grid=(n // tm,),
        in_specs=[
            pl.BlockSpec((tm, D_MODEL), lambda i: (i, 0)),
            pl.BlockSpec((1, D_MODEL), const),
            pl.BlockSpec(w.shape, const),
            pl.BlockSpec(bd.shape, const),
            pl.BlockSpec((1, HEAD_BLOCK), const),
            pl.BlockSpec((1, HEAD_BLOCK), const),
            pl.BlockSpec((tm, HEAD_BLOCK), lambda i: (i % n_seq_tiles, 0)),
            pl.BlockSpec((tm, HEAD_BLOCK), lambda i: (i % n_seq_tiles, 0)),
        ],
        out_specs=(pl.BlockSpec((tm, C_Q_WIDTH), lambda i: (i, 0)),
                   pl.BlockSpec((tm, 2 * C_KV_WIDTH), lambda i: (i, 0))),
        compiler_params=_cparams(("arbitrary",)),
        name="proj_odd",
    )(x, ln, w, bd, qn, kn, cos, sin)


def _dilated_body(q_ref, k_ref, v_ref, o_ref, lse_ref, *, tq, kw, sub_len):
    i = pl.program_id(2)
    q0 = i * tq
    k0 = pl.multiple_of(jnp.clip(q0 - A_HALF, 0, sub_len - kw), A_HALF)
    q = q_ref[...]
    kwin = k_ref[pl.ds(k0, kw), :]
    vwin = v_ref[pl.ds(k0, kw), :]
    qpos = q0 + lax.broadcasted_iota(jnp.int32, (tq, kw), 0)
    kpos = k0 + lax.broadcasted_iota(jnp.int32, (tq, kw), 1)
    valid = jnp.abs(kpos - qpos) <= A_HALF
    lane_head = lax.broadcasted_iota(jnp.int32, (tq, HEAD_BLOCK), 1) // HEAD_DIM
    o = jnp.zeros((tq, HEAD_BLOCK), F32)
    lse = jnp.zeros((tq, HEAD_BLOCK), F32)
    for h in range(HEADS_PER_BLOCK):
        sel = lane_head == h
        qh = jnp.where(sel, q, jnp.zeros_like(q))
        s = lax.dot_general(qh, kwin, (((1,), (1,)), ((), ())), preferred_element_type=F32)
        s = jnp.where(valid, s, NEG_INF)
        m = jnp.max(s, axis=-1, keepdims=True)
        e = jnp.exp(s - m)
        den = jnp.sum(e, axis=-1, keepdims=True)
        p = (e / den).astype(BF16)
        oh = jnp.dot(p, vwin, preferred_element_type=F32)
        o = jnp.where(sel, oh, o)
        lse = jnp.where(sel, m + jnp.log(den), lse)
    o_ref[...] = o.astype(BF16)
    lse_ref[...] = lse


def _dilated_attention(qkv, batch, seq, group):
    _, dil = A_GROUPS[group]
    sub_len = seq // dil
    tq = min(128, sub_len)
    kw = min(256, sub_len)
    n_blk = QKV_E // HEAD_BLOCK
    view = qkv.reshape(batch, sub_len, dil * QKV_E)
    heads = A_WIDTH // HEAD_BLOCK
    o, lse = pl.pallas_call(
        functools.partial(_dilated_body, tq=tq, kw=kw, sub_len=sub_len),
        out_shape=(jax.ShapeDtypeStruct((batch, sub_len, dil * HEAD_BLOCK), BF16),
                   jax.ShapeDtypeStruct((batch, sub_len, dil * HEAD_BLOCK), F32)),
        grid=(batch, dil, sub_len // tq),
        in_specs=[
            pl.BlockSpec((None, tq, HEAD_BLOCK), lambda b, r, i: (b, i, r * n_blk + group)),
            pl.BlockSpec((None, sub_len, HEAD_BLOCK), lambda b, r, i: (b, 0, r * n_blk + heads + group)),
            pl.BlockSpec((None, sub_len, HEAD_BLOCK), lambda b, r, i: (b, 0, r * n_blk + 2 * heads + group)),
        ],
        out_specs=(pl.BlockSpec((None, tq, HEAD_BLOCK), lambda b, r, i: (b, i, r)),
                   pl.BlockSpec((None, tq, HEAD_BLOCK), lambda b, r, i: (b, i, r))),
        compiler_params=_cparams(("arbitrary", "arbitrary", "arbitrary")),
        name=f"dilated_attn_g{group}",
    )(view, view, view)
    return o.reshape(batch * seq, HEAD_BLOCK), lse.reshape(batch * seq, HEAD_BLOCK)


def _ssm_body(u_ref, bmat_ref, apow_ref, ptab_ref, cmat_ref, y_ref, carry_ref, *, tm, reverse):
    @pl.when(pl.program_id(1) == 0)
    def _():
        carry_ref[...] = jnp.zeros_like(carry_ref)

    u = u_ref[...].astype(BF16)
    bu = jnp.dot(u, bmat_ref[...], preferred_element_type=F32)
    row = lax.broadcasted_iota(jnp.int32, (tm, LANES), 0)
    n_steps = int(math.log2(tm))
    y = jnp.zeros((tm, B_WIDTH), F32)
    for j in range(SSM_N // LANES):
        lanes = slice(j * LANES, (j + 1) * LANES)
        hr = bu[:, j * LANES:(j + 1) * LANES]
        hi = bu[:, SSM_N + j * LANES:SSM_N + (j + 1) * LANES]
        for s in range(n_steps):
            k = 1 << s
            ar = apow_ref[s:s + 1, lanes]
            ai = apow_ref[n_steps + s:n_steps + s + 1, lanes]
            if reverse:
                keep = row < tm - k
                sr = jnp.where(keep, pltpu.roll(hr, tm - k, 0), 0.0)
                si = jnp.where(keep, pltpu.roll(hi, tm - k, 0), 0.0)
            else:
                keep = row >= k
                sr = jnp.where(keep, pltpu.roll(hr, k, 0), 0.0)
                si = jnp.where(keep, pltpu.roll(hi, k, 0), 0.0)
            hr, hi = hr + ar * sr - ai * si, hi + ar * si + ai * sr
        cr = carry_ref[0:1, lanes]
        ci = carry_ref[1:2, lanes]
        pr = ptab_ref[:, lanes]
        pi = ptab_ref[:, SSM_N + j * LANES:SSM_N + (j + 1) * LANES]
        hr, hi = hr + pr * cr - pi * ci, hi + pr * ci + pi * cr
        edge = 0 if reverse else tm - 1
        carry_ref[0:1, lanes] = hr[edge:edge + 1, :]
        carry_ref[1:2, lanes] = hi[edge:edge + 1, :]
        y = y + jnp.dot(hr.astype(BF16), cmat_ref[lanes, :], preferred_element_type=F32)
        y = y + jnp.dot(hi.astype(BF16), cmat_ref[SSM_N + j * LANES:SSM_N + (j + 1) * LANES, :],
                        preferred_element_type=F32)
    y_ref[...] = y


def _ssm_scan(u, bmat, apow, ptab, cmat, batch, seq, tm, reverse):
    n_t = seq // tm
    tmap = (lambda b, i: (b, n_t - 1 - i, 0)) if reverse else (lambda b, i: (b, i, 0))
    const = lambda b, i: (0, 0)
    return pl.pallas_call(
        functools.partial(_ssm_body, tm=tm, reverse=reverse),
        out_shape=jax.ShapeDtypeStruct((batch, seq, B_WIDTH), F32),
        grid=(batch, n_t),
        in_specs=[
            pl.BlockSpec((None, tm, B_WIDTH), tmap),
            pl.BlockSpec(bmat.shape, const),
            pl.BlockSpec(apow.shape, const),
            pl.BlockSpec(ptab.shape, const),
            pl.BlockSpec(cmat.shape, const),
        ],
        out_specs=pl.BlockSpec((None, tm, B_WIDTH), tmap),
        scratch_shapes=[pltpu.VMEM((8, SSM_N), F32)],
        compiler_params=_cparams(("arbitrary", "arbitrary")),
        name="ssm_scan_rev" if reverse else "ssm_scan_fwd",
    )(u.reshape(batch, seq, B_WIDTH), bmat, apow, ptab, cmat).reshape(batch * seq, B_WIDTH)


def _ssm_tables(lam_re, lam_im, log_dt, b_re, b_im, c_re, c_im, tm, reverse):
    lr, li = lam_re.astype(F32), lam_im.astype(F32)
    dt = jnp.exp(log_dt.astype(F32))[:, None]
    mag = jnp.exp(lr * dt)
    a_r = mag * jnp.cos(li * dt)
    a_i = mag * jnp.sin(li * dt)
    den = lr * lr + li * li
    f_r = ((a_r - 1.0) * lr + a_i * li) / den
    f_i = (a_i * lr - (a_r - 1.0) * li) / den
    br, bi = b_re.astype(F32), b_im.astype(F32)
    bb_r = f_r[..., None] * br - f_i[..., None] * bi
    bb_i = f_r[..., None] * bi + f_i[..., None] * br
    eye = jnp.eye(B_GROUPS, dtype=F32)

    def in_mat(bb):
        return jnp.einsum('gpc,gh->gchp', bb, eye).reshape(B_WIDTH, SSM_N)

    def out_mat(c):
        return jnp.einsum('gcp,gh->gphc', c.astype(F32), eye).reshape(SSM_N, B_WIDTH)

    bmat = jnp.concatenate([in_mat(bb_r), in_mat(bb_i)], axis=1).astype(BF16)
    cmat = jnp.concatenate([out_mat(c_re), -out_mat(c_im)], axis=0).astype(BF16)

    def cpow(k):
        kk = k.astype(F32)[:, None]
        m = jnp.exp(kk * (lr * dt).reshape(1, SSM_N))
        ang = kk * (li * dt).reshape(1, SSM_N)
        return m * jnp.cos(ang), m * jnp.sin(ang)

    n_steps = int(math.log2(tm))
    sr, si = cpow(2 ** jnp.arange(n_steps))
    apow = jnp.concatenate([sr, si], axis=0)
    t = jnp.arange(tm)
    pr, pi = cpow(tm - t if reverse else t + 1)
    ptab = jnp.concatenate([pr, pi], axis=1)
    return bmat, apow, ptab, cmat


def _gelu_tanh(x):
    return 0.5 * x * (1.0 + jnp.tanh(math.sqrt(2.0 / math.pi) * (x + 0.044715 * (x * x * x))))


def _even_out_body(x_ref, o0_ref, o1_ref, o2_ref, l0_ref, l1_ref, l2_ref, u_ref, yf_ref, yb_ref,
                   d_ref, wglu_ref, wout_ref, out_ref):
    l0, l1, l2 = l0_ref[...], l1_ref[...], l2_ref[...]
    lmax = jnp.maximum(jnp.maximum(l0, l1), l2)
    e0, e1, e2 = jnp.exp(l0 - lmax), jnp.exp(l1 - lmax), jnp.exp(l2 - lmax)
    den = e0 + e1 + e2
    a = ((e0 / den) * o0_ref[...].astype(F32) + (e1 / den) * o1_ref[...].astype(F32)
         + (e2 / den) * o2_ref[...].astype(F32))
    y = u_ref[...] * d_ref[...] + yf_ref[...] + yb_ref[...]
    z = _gelu_tanh(y)
    z = z * jax.nn.sigmoid(jnp.dot(z.astype(BF16), wglu_ref[...], preferred_element_type=F32))
    acc = jnp.dot(a.astype(BF16), wout_ref[0:HEAD_BLOCK, :], preferred_element_type=F32)
    acc = acc + jnp.dot(z.astype(BF16), wout_ref[HEAD_BLOCK:, :], preferred_element_type=F32)
    out_ref[...] = x_ref[...] + acc


def _even_out(x, os_, ls_, u, yf, yb, d_skip, wglu, wout, tm):
    n = x.shape[0]
    row = lambda i: (i, 0)
    const = lambda i: (0, 0)
    narrow = pl.BlockSpec((tm, HEAD_BLOCK), row)
    return pl.pallas_call(
        _even_out_body,
        out_shape=jax.ShapeDtypeStruct((n, D_MODEL), F32),
        grid=(n // tm,),
        in_specs=[pl.BlockSpec((tm, D_MODEL), row)] + [narrow] * 9 + [
            pl.BlockSpec((1, B_WIDTH), const),
            pl.BlockSpec(wglu.shape, const),
            pl.BlockSpec(wout.shape, const),
        ],
        out_specs=pl.BlockSpec((tm, D_MODEL), row),
        compiler_params=_cparams(("arbitrary",)),
        name="even_out",
    )(x, *os_, *ls_, u, yf, yb, d_skip, wglu, wout)


def _ffn_body(x_ref, g_ref, wg_ref, wu_ref, wd_ref, out_ref, *, chunk):
    x = x_ref[...]
    xn = _rms(x, g_ref[...]).astype(BF16)
    acc = jnp.zeros(x.shape, F32)
    for c in range(D_FF // chunk):
        cols = slice(c * chunk, (c + 1) * chunk)
        g = jnp.dot(xn, wg_ref[:, cols], preferred_element_type=F32)
        up = jnp.dot(xn, wu_ref[:, cols], preferred_element_type=F32)
        h = (g * jax.nn.sigmoid(g) * up).astype(BF16)
        acc = acc + jnp.dot(h, wd_ref[cols, :], preferred_element_type=F32)
    out_ref[...] = x + acc


def _ffn(x, ln, wg, wu, wd, tm):
    n = x.shape[0]
    row = lambda i: (i, 0)
    const = lambda i: (0, 0)
    return pl.pallas_call(
        functools.partial(_ffn_body, chunk=256),
        out_shape=jax.ShapeDtypeStruct((n, D_MODEL), F32),
        grid=(n // tm,),
        in_specs=[
            pl.BlockSpec((tm, D_MODEL), row),
            pl.BlockSpec((1, D_MODEL), const),
            pl.BlockSpec(wg.shape, const),
            pl.BlockSpec(wu.shape, const),
            pl.BlockSpec(wd.shape, const),
        ],
        out_specs=pl.BlockSpec((tm, D_MODEL), row),
        compiler_params=_cparams(("arbitrary",)),
        name="ffn_dense",
    )(x, ln, wg, wu, wd)


def _ple_body(x_ref, p_ref, g_ref, wp_ref, wg_ref, out_ref):
    x = x_ref[...]
    xn = _rms(x, g_ref[...]).astype(BF16)
    gate = jax.nn.sigmoid(jnp.dot(xn, wg_ref[...], preferred_element_type=F32))
    emb = jnp.dot(p_ref[...].astype(BF16), wp_ref[...], preferred_element_type=F32)
    out_ref[...] = x + emb * gate


def _ple(x, p, ln, wp, wg, tm):
    n = x.shape[0]
    row = lambda i: (i, 0)
    const = lambda i: (0, 0)
    return pl.pallas_call(
        _ple_body,
        out_shape=jax.ShapeDtypeStruct((n, D_MODEL), F32),
        grid=(n // tm,),
        in_specs=[
            pl.BlockSpec((tm, D_MODEL), row),
            pl.BlockSpec((tm, PLE_DIM), row),
            pl.BlockSpec((1, D_MODEL), const),
            pl.BlockSpec(wp.shape, const),
            pl.BlockSpec(wg.shape, const),
        ],
        out_specs=pl.BlockSpec((tm, D_MODEL), row),
        compiler_params=_cparams(("arbitrary",)),
        name="per_layer_embed",
    )(x, p, ln, wp, wg)


def _gqa_body(q_ref, k_ref, v_ref, o_ref, *, tq, tk, seq):
    lane_head = lax.broadcasted_iota(jnp.int32, (tq, HEAD_BLOCK), 1) // HEAD_DIM
    n_kv = seq // tk
    for c in range(C_Q_WIDTH // HEAD_BLOCK):
        q = q_ref[:, c * HEAD_BLOCK:(c + 1) * HEAD_BLOCK]
        out = jnp.zeros((tq, HEAD_BLOCK), F32)
        for g in range(HEADS_PER_BLOCK):
            sel = lane_head == g
            qg = jnp.where(sel, q, jnp.zeros_like(q))

            def step(t, carry, qg=qg):
                m, l, acc = carry
                k0 = pl.multiple_of(t * tk, tk)
                kt = k_ref[pl.ds(k0, tk), :]
                vt = v_ref[pl.ds(k0, tk), :]
                s = lax.dot_general(qg, kt, (((1,), (1,)), ((), ())), preferred_element_type=F32)
                m_new = jnp.maximum(m, jnp.max(s, axis=-1, keepdims=True))
                alpha = jnp.exp(m - m_new)
                p = jnp.exp(s - m_new)
                l_new = alpha * l + jnp.sum(p, axis=-1, keepdims=True)
                acc_new = alpha * acc + jnp.dot(p.astype(BF16), vt, preferred_element_type=F32)
                return m_new, l_new, acc_new

            init = (jnp.full((tq, 1), NEG_INF, F32), jnp.zeros((tq, 1), F32), jnp.zeros((tq, HEAD_BLOCK), F32))
            m, l, acc = lax.fori_loop(0, n_kv, step, init)
            out = jnp.where(sel, acc / l, out)
        o_ref[:, c * HEAD_BLOCK:(c + 1) * HEAD_BLOCK] = out.astype(BF16)


def _gqa_attention(q, kv, batch, seq, tq, tk):
    return pl.pallas_call(
        functools.partial(_gqa_body, tq=tq, tk=tk, seq=seq),
        out_shape=jax.ShapeDtypeStruct((batch, seq, C_Q_WIDTH), BF16),
        grid=(batch, seq // tq),
        in_specs=[
            pl.BlockSpec((None, tq, C_Q_WIDTH), lambda b, i: (b, i, 0)),
            pl.BlockSpec((None, seq, C_KV_WIDTH), lambda b, i: (b, 0, 0)),
            pl.BlockSpec((None, seq, C_KV_WIDTH), lambda b, i: (b, 0, 1)),
        ],
        out_specs=pl.BlockSpec((None, tq, C_Q_WIDTH), lambda b, i: (b, i, 0)),
        compiler_params=_cparams(("arbitrary", "arbitrary")),
        name="gqa_attention",
    )(q.reshape(batch, seq, C_Q_WIDTH), kv.reshape(batch, seq, 2 * C_KV_WIDTH),
      kv.reshape(batch, seq, 2 * C_KV_WIDTH)).reshape(batch * seq, C_Q_WIDTH)


def _odd_out_body(x_ref, a_ref, wout_ref, g_ref, rw_hi_ref, rw_lo_ref, out_ref, gates_ref):
    x = x_ref[...] + jnp.dot(a_ref[...], wout_ref[...], preferred_element_type=F32)
    out_ref[...] = x
    h = _rms(x, g_ref[...])
    h_hi = h.astype(BF16)
    h_lo = (h - h_hi.astype(F32)).astype(BF16)
    logits = (jnp.dot(h_hi, rw_hi_ref[...], preferred_element_type=F32)
              + jnp.dot(h_hi, rw_lo_ref[...], preferred_element_type=F32)
              + jnp.dot(h_lo, rw_hi_ref[...], preferred_element_type=F32))
    lane = lax.broadcasted_iota(jnp.int32, logits.shape, 1)
    logits = jnp.where(lane < N_EXPERTS, logits, -jnp.inf)
    m1 = jnp.max(logits, axis=-1, keepdims=True)
    i1 = jnp.min(jnp.where(logits == m1, lane, LANES), axis=-1, keepdims=True)
    rest = jnp.where(lane == i1, -jnp.inf, logits)
    m2 = jnp.max(rest, axis=-1, keepdims=True)
    i2 = jnp.min(jnp.where(rest == m2, lane, LANES), axis=-1, keepdims=True)
    e2 = jnp.exp(m2 - m1)
    w1 = 1.0 / (1.0 + e2)
    w2 = e2 / (1.0 + e2)
    gates_ref[...] = jnp.where(lane == i1, w1, 0.0) + jnp.where(lane == i2, w2, 0.0)


def _odd_out(x, attn, wout, ln, rw_hi, rw_lo, tm):
    n = x.shape[0]
    row = lambda i: (i, 0)
    const = lambda i: (0, 0)
    return pl.pallas_call(
        _odd_out_body,
        out_shape=(jax.ShapeDtypeStruct((n, D_MODEL), F32), jax.ShapeDtypeStruct((n, LANES), F32)),
        grid=(n // tm,),
        in_specs=[
            pl.BlockSpec((tm, D_MODEL), row),
            pl.BlockSpec((tm, C_Q_WIDTH), row),
            pl.BlockSpec(wout.shape, const),
            pl.BlockSpec((1, D_MODEL), const),
            pl.BlockSpec(rw_hi.shape, const),
            pl.BlockSpec(rw_lo.shape, const),
        ],
        out_specs=(pl.BlockSpec((tm, D_MODEL), row), pl.BlockSpec((tm, LANES), row)),
        compiler_params=_cparams(("arbitrary",)),
        name="odd_out_router",
    )(x, attn, wout, ln, rw_hi, rw_lo)


def _moe_body(x_ref, gates_ref, g_ref, wg_ref, wu_ref, wd_ref, out_ref, xn_ref, acc_ref):
    e = pl.program_id(1)
    c = pl.program_id(2)

    @pl.when((e == 0) & (c == 0))
    def _():
        xn_ref[...] = _rms(x_ref[...], g_ref[...]).astype(BF16)
        acc_ref[...] = jnp.zeros_like(acc_ref)

    xn = xn_ref[...]
    g = jnp.dot(xn, wg_ref[...], preferred_element_type=F32)
    up = jnp.dot(xn, wu_ref[...], preferred_element_type=F32)
    gates = gates_ref[...]
    lane = lax.broadcasted_iota(jnp.int32, gates.shape, 1)
    ge = jnp.sum(jnp.where(lane == e, gates, 0.0), axis=-1, keepdims=True)
    h = (g * jax.nn.sigmoid(g) * up * ge).astype(BF16)
    acc_ref[...] += jnp.dot(h, wd_ref[...], preferred_element_type=F32)

    @pl.when((e == pl.num_programs(1) - 1) & (c == pl.num_programs(2) - 1))
    def _():
        out_ref[...] = x_ref[...] + acc_ref[...]


def _moe_dense(x, gates, ln, wg, wu, wd, tm, chunk):
    n = x.shape[0]
    row = lambda i, e, c: (i, 0)
    return pl.pallas_call(
        _moe_body,
        out_shape=jax.ShapeDtypeStruct((n, D_MODEL), F32),
        grid=(n // tm, N_EXPERTS, D_EXPERT // chunk),
        in_specs=[
            pl.BlockSpec((tm, D_MODEL), row),
            pl.BlockSpec((tm, LANES), row),
            pl.BlockSpec((1, D_MODEL), lambda i, e, c: (0, 0)),
            pl.BlockSpec((None, D_MODEL, chunk), lambda i, e, c: (e, 0, c)),
            pl.BlockSpec((None, D_MODEL, chunk), lambda i, e, c: (e, 0, c)),
            pl.BlockSpec((None, chunk, D_MODEL), lambda i, e, c: (e, c, 0)),
        ],
        out_specs=pl.BlockSpec((tm, D_MODEL), row),
        scratch_shapes=[pltpu.VMEM((tm, D_MODEL), BF16), pltpu.VMEM((tm, D_MODEL), F32)],
        compiler_params=_cparams(("arbitrary", "arbitrary", "arbitrary")),
        name="moe_dense",
    )(x, gates, ln, wg, wu, wd)


def _rope_tables(pos, dim):
    inv = ROPE_THETA ** (-jnp.arange(0, dim, 2, dtype=F32) / dim)
    ang = pos.astype(F32)[:, None] * inv[None, :]
    ang = jnp.concatenate([ang, ang], axis=-1)
    return jnp.cos(ang), jnp.sin(ang)


def _tile_heads(t):
    return jnp.tile(t, (1, HEADS_PER_BLOCK))


def _even_rope(seq):
    cos, sin = _rope_tables(jnp.arange(seq), HEAD_DIM)
    sign = jnp.where(jnp.arange(HEAD_DIM) < HEAD_DIM // 2, -1.0, 1.0).astype(F32)
    return _tile_heads(cos), _tile_heads(sin * sign)


def _odd_rope(seq):
    t = jnp.arange(seq)
    half = HEAD_DIM // 2
    cr, sr = _rope_tables(t // GRID_W, half)
    cc, sc = _rope_tables(t % GRID_W, half)
    sign = jnp.where(jnp.arange(half) < half // 2, -1.0, 1.0).astype(F32)
    cos = jnp.concatenate([cr, cc], axis=-1)
    sin = jnp.concatenate([sr * sign, sc * sign], axis=-1)
    return _tile_heads(cos), _tile_heads(sin)


def _head_mean_matrix():
    blk = np.arange(HEAD_BLOCK) // HEAD_DIM
    return jnp.asarray((blk[:, None] == blk[None, :]).astype(np.float32) / HEAD_DIM, dtype=BF16)


def _q_head_permutation():
    n_blocks = C_Q_WIDTH // HEAD_BLOCK
    heads = [HEADS_PER_BLOCK * g + c for c in range(n_blocks) for g in range(HEADS_PER_BLOCK)]
    return np.concatenate([np.arange(h * HEAD_DIM, (h + 1) * HEAD_DIM) for h in heads])


def _row(v):
    return v.astype(F32).reshape(1, -1)


def _even_layer(x, batch, seq, w):
    tm = 512
    cos, sin = _even_rope(seq)
    qkv, u = _proj_even(x, w['ln_mix'], w['w_in'], w['bd'], w['qn'], w['kn'], cos, sin, seq, tm)
    outs, lses = [], []
    for g in range(len(A_GROUPS)):
        o, lse = _dilated_attention(qkv, batch, seq, g)
        outs.append(o)
        lses.append(lse)
    ssm_tm = 128
    ys = []
    for direction in range(2):
        tabs = _ssm_tables(*(w[k][direction] for k in ('lam_re', 'lam_im', 'log_dt', 'b_re', 'b_im', 'c_re', 'c_im')),
                           ssm_tm, direction == 1)
        ys.append(_ssm_scan(u, *tabs, batch, seq, ssm_tm, direction == 1))
    x = _even_out(x, outs, lses, u, ys[0], ys[1], w['d_skip'], w['w_glu'], w['w_out'], tm)
    return _ffn(x, w['ln_ffn'], w['ffn_gate'], w['ffn_up'], w['ffn_down'], 256)


def _odd_layer(x, batch, seq, w):
    tm = 512
    cos, sin = _odd_rope(seq)
    q, kv = _proj_odd(x, w['ln_mix'], w['w_in'], w['bd'], w['qn'], w['kn'], cos, sin, seq, tm)
    attn = _gqa_attention(q, kv, batch, seq, 256, 512)
    x, gates = _odd_out(x, attn, w['w_out'], w['ln_ffn'], w['router_hi'], w['router_lo'], tm)
    return _moe_dense(x, gates, w['ln_ffn'], w['moe_gate'], w['moe_up'], w['moe_down'], 1024, 512)


def _prepare_even(j, ln_mix_e, w_in_e, a_qnorm, a_knorm, lam_re, lam_im, log_dt, b_re, b_im, c_re, c_im,
                  ssm_d, ssm_w_glu, w_out_e, ln_ffn_e, ffn_w_gate, ffn_w_up, ffn_w_down):
    return dict(
        ln_mix=_row(ln_mix_e[j]), w_in=w_in_e[j].astype(BF16), bd=_head_mean_matrix(),
        qn=_tile_heads(_row(a_qnorm[j])), kn=_tile_heads(_row(a_knorm[j])),
        lam_re=lam_re[j], lam_im=lam_im[j], log_dt=log_dt[j], b_re=b_re[j], b_im=b_im[j],
        c_re=c_re[j], c_im=c_im[j], d_skip=_row(ssm_d[j]), w_glu=ssm_w_glu[j].astype(BF16),
        w_out=w_out_e[j].astype(BF16), ln_ffn=_row(ln_ffn_e[j]), ffn_gate=ffn_w_gate[j].astype(BF16),
        ffn_up=ffn_w_up[j].astype(BF16), ffn_down=ffn_w_down[j].astype(BF16))


def _prepare_odd(j, ln_mix_o, w_in_o, c_qnorm, c_knorm, w_out_o, ln_ffn_o, router_w, moe_w_gate, moe_w_up,
                 moe_w_down):
    perm = _q_head_permutation()
    w_in = w_in_o[j]
    w_in = jnp.concatenate([w_in[:, :C_Q_WIDTH][:, perm], w_in[:, C_Q_WIDTH:]], axis=1).astype(BF16)
    rw = jnp.pad(router_w[j].astype(F32), ((0, 0), (0, LANES - N_EXPERTS)))
    rw_hi = rw.astype(BF16)
    rw_lo = (rw - rw_hi.astype(F32)).astype(BF16)
    return dict(
        ln_mix=_row(ln_mix_o[j]), w_in=w_in, bd=_head_mean_matrix(),
        qn=_tile_heads(_row(c_qnorm[j])), kn=_tile_heads(_row(c_knorm[j])),
        w_out=w_out_o[j][perm, :].astype(BF16), ln_ffn=_row(ln_ffn_o[j]), router_hi=rw_hi, router_lo=rw_lo,
        moe_gate=moe_w_gate[j].astype(BF16), moe_up=moe_w_up[j].astype(BF16), moe_down=moe_w_down[j].astype(BF16))


def _trunk(x, p, layers, ple):
    batch, seq, _ = x.shape
    x = x.reshape(batch * seq, D_MODEL)
    for i, w in enumerate(layers):
        x = (_even_layer if i % 2 == 0 else _odd_layer)(x, batch, seq, w)
        x = _ple(x, p[i].reshape(batch * seq, PLE_DIM), *ple[i], 512)
    return x.reshape(batch, seq, D_MODEL)


def kernel(x_prompt, x_sample, p_prompt, p_sample, ln_mix_e, w_in_e, a_qnorm, a_knorm, ssm_lam_re, ssm_lam_im, ssm_log_dt, ssm_b_re, ssm_b_im, ssm_c_re, ssm_c_im, ssm_d, ssm_w_glu, w_out_e, ln_ffn_e, ffn_w_gate, ffn_w_up, ffn_w_down, ln_mix_o, w_in_o, c_qnorm, c_knorm, w_out_o, ln_ffn_o, router_w, moe_w_gate, moe_w_up, moe_w_down, ple_ln, ple_w_proj, ple_w_gate):
    depth = p_prompt.shape[0]
    layers = []
    for i in range(depth):
        j = i // 2
        if i % 2 == 0:
            layers.append(_prepare_even(j, ln_mix_e, w_in_e, a_qnorm, a_knorm, ssm_lam_re, ssm_lam_im, ssm_log_dt,
                                        ssm_b_re, ssm_b_im, ssm_c_re, ssm_c_im, ssm_d, ssm_w_glu, w_out_e, ln_ffn_e,
                                        ffn_w_gate, ffn_w_up, ffn_w_down))
        else:
            layers.append(_prepare_odd(j, ln_mix_o, w_in_o, c_qnorm, c_knorm, w_out_o, ln_ffn_o, router_w,
                                       moe_w_gate, moe_w_up, moe_w_down))
    ple = [(_row(ple_ln[i]), ple_w_proj[i].astype(BF16), ple_w_gate[i].astype(BF16)) for i in range(depth)]
    y_prompt = _trunk(x_prompt, p_prompt, layers, ple)
    y_sample = _trunk(x_sample, p_sample, layers, ple)
    return (y_prompt, y_sample)
```

```python
import functools
import math

import jax
import jax.numpy as jnp
import numpy as np
from jax import lax
from jax.experimental import pallas as pl
from jax.experimental.pallas import tpu as pltpu

F32 = jnp.float32
BF16 = jnp.bfloat16

D_MODEL = 1024
HEAD_DIM = 64
EPS = 1e-6
ROPE_THETA = 10000.0
NEG_INF = -1e30
LOG2_E = math.log2(math.e)
LANES = 128
HEAD_BLOCK = 256
HEADS_PER_BLOCK = HEAD_BLOCK // HEAD_DIM
A_GROUPS = ((128, 1), (512, 4), (2048, 16))
A_HALF = 64
A_WIDTH = 768
B_WIDTH = 256
B_GROUP_CH = 16
B_GROUPS = 16
B_STATE = 64
SSM_N = B_GROUPS * B_STATE
QKV_E = 3 * A_WIDTH
D_FF = 2816
GRID_W = 64
C_Q_WIDTH = 1024
C_KV_WIDTH = 256
V_ONES_ROWS = 16
V_HEAD_ROWS = HEAD_DIM + V_ONES_ROWS
N_EXPERTS = 8
D_EXPERT = 3584
PLE_DIM = 256

VMEM_LIMIT = 56 * 1024 * 1024


def _cparams(sem):
    return pltpu.CompilerParams(dimension_semantics=sem, vmem_limit_bytes=VMEM_LIMIT)


def _rms(x, g):
    ms = jnp.mean(x * x, axis=-1, keepdims=True)
    return x * lax.rsqrt(ms + EPS) * g


def _headnorm_rope(y, bd, gain, cos, sin_signed, half):
    ss = jnp.dot((y * y).astype(BF16), bd, preferred_element_type=F32)
    yn = y * lax.rsqrt(ss + EPS) * gain
    width = yn.shape[1]
    lane = lax.broadcasted_iota(jnp.int32, yn.shape, 1)
    up = pltpu.roll(yn, width - half, 1)
    dn = pltpu.roll(yn, half, 1)
    rot = jnp.where((lane & (2 * half - 1)) < half, up, dn)
    return yn * cos + rot * sin_signed


def _proj_even_body(x_ref, g_ref, w_ref, bd_ref, qn_ref, kn_ref, cos_ref, sin_ref, qkv_ref, u_ref):
    xn = _rms(x_ref[...], g_ref[...]).astype(BF16)
    n_head_blocks = A_WIDTH // HEAD_BLOCK
    for c in range(QKV_E // HEAD_BLOCK + 1):
        y = jnp.dot(xn, w_ref[:, c * HEAD_BLOCK:(c + 1) * HEAD_BLOCK], preferred_element_type=F32)
        if c < 2 * n_head_blocks:
            gain = qn_ref[...] if c < n_head_blocks else kn_ref[...]
            y = _headnorm_rope(y, bd_ref[...], gain, cos_ref[...], sin_ref[...], HEAD_DIM // 2)
            if c < n_head_blocks:
                y = y * (HEAD_DIM ** -0.5)
        if c < QKV_E // HEAD_BLOCK:
            qkv_ref[:, c * HEAD_BLOCK:(c + 1) * HEAD_BLOCK] = y.astype(BF16)
        else:
            u_ref[...] = y


def _proj_even(x, ln, w, bd, qn, kn, cos, sin, seq, tm):
    n = x.shape[0]
    n_seq_tiles = seq // tm
    const = lambda i: (0, 0)
    return pl.pallas_call(
        _proj_even_body,
        out_shape=(jax.ShapeDtypeStruct((n, QKV_E), BF16), jax.ShapeDtypeStruct((n, B_WIDTH), F32)),
        grid=(n // tm,),
        in_specs=[
            pl.BlockSpec((tm, D_MODEL), lambda i: (i, 0)),
            pl.BlockSpec((1, D_MODEL), const),
            pl.BlockSpec(w.shape, const),
            pl.BlockSpec(bd.shape, const),
            pl.BlockSpec((1, HEAD_BLOCK), const),
            pl.BlockSpec((1, HEAD_BLOCK), const),
            pl.BlockSpec((tm, HEAD_BLOCK), lambda i: (i % n_seq_tiles, 0)),
            pl.BlockSpec((tm, HEAD_BLOCK), lambda i: (i % n_seq_tiles, 0)),
        ],
        out_specs=(pl.BlockSpec((tm, QKV_E), lambda i: (i, 0)), pl.BlockSpec((tm, B_WIDTH), lambda i: (i, 0))),
        compiler_params=_cparams(("arbitrary",)),
        name="proj_even",
    )(x, ln, w, bd, qn, kn, cos, sin)


def _proj_odd_body(x_ref, g_ref, w_ref, bd_ref, qn_ref, kn_ref, cos_ref, sin_ref, qt_ref, k_ref, vt_ref):
    xn = _rms(x_ref[...], g_ref[...]).astype(BF16)
    tm = xn.shape[0]
    n_q_blocks = C_Q_WIDTH // HEAD_BLOCK
    for c in range(n_q_blocks + 2):
        y = jnp.dot(xn, w_ref[:, c * HEAD_BLOCK:(c + 1) * HEAD_BLOCK], preferred_element_type=F32)
        if c <= n_q_blocks:
            gain = qn_ref[...] if c < n_q_blocks else kn_ref[...]
            y = _headnorm_rope(y, bd_ref[...], gain, cos_ref[...], sin_ref[...], HEAD_DIM // 4)
        if c < n_q_blocks:
            qt_ref[c * HEAD_BLOCK:(c + 1) * HEAD_BLOCK, :] = (y * (HEAD_DIM ** -0.5 * LOG2_E)).T.astype(BF16)
        elif c == n_q_blocks:
            k_ref[...] = y.astype(BF16)
        else:
            yt = y.T.astype(BF16)
            ones = jnp.ones((V_ONES_ROWS, tm), BF16)
            for g in range(HEADS_PER_BLOCK):
                vt_ref[g * V_HEAD_ROWS:g * V_HEAD_ROWS + HEAD_DIM, :] = yt[g * HEAD_DIM:(g + 1) * HEAD_DIM, :]
                vt_ref[g * V_HEAD_ROWS + HEAD_DIM:(g + 1) * V_HEAD_ROWS, :] = ones


def _proj_odd(x, ln, w, bd, qn, kn, cos, sin, seq, tm):
    n = x.shape[0]
    n_seq_tiles = seq // tm
    const = lambda i: (0, 0)
    return pl.pallas_call(
        _proj_odd_body,
        out_shape=(jax.ShapeDtypeStruct((n // tm, C_Q_WIDTH, tm), BF16),
                   jax.ShapeDtypeStruct((n, C_KV_WIDTH), BF16),
                   jax.ShapeDtypeStruct((n // tm, HEADS_PER_BLOCK * V_HEAD_ROWS, tm), BF16)),
        grid=(n // tm,),
        in_specs=[
            pl.BlockSpec((tm, D_MODEL), lambda i: (i, 0)),
            pl.BlockSpec((1, D_MODEL), const),
            pl.BlockSpec(w.shape, const),
            pl.BlockSpec(bd.shape, const),
            pl.BlockSpec((1, HEAD_BLOCK), const),
            pl.BlockSpec((1, HEAD_BLOCK), const),
            pl.BlockSpec((tm, HEAD_BLOCK), lambda i: (i % n_seq_tiles, 0)),
            pl.BlockSpec((tm, HEAD_BLOCK), lambda i: (i % n_seq_tiles, 0)),
        ],
        out_specs=(pl.BlockSpec((None, C_Q_WIDTH, tm), lambda i: (i, 0, 0)),
                   pl.BlockSpec((tm, C_KV_WIDTH), lambda i: (i, 0)),
                   pl.BlockSpec((None, HEADS_PER_BLOCK * V_HEAD_ROWS, tm), lambda i: (i, 0, 0))),
        compiler_params=_cparams(("arbitrary",)),
        name="proj_odd",
    )(x, ln, w, bd, qn, kn, cos, sin)


def _dilated_body(q_ref, k_ref, v_ref, o_ref, lse_ref, *, tq, kw, sub_len):
    i = pl.program_id(2)
    q0 = i * tq
    k0 = pl.multiple_of(jnp.clip(q0 - A_HALF, 0, sub_len - kw), A_HALF)
    q = q_ref[...]
    kwin = k_ref[pl.ds(k0, kw), :]
    vwin = v_ref[pl.ds(k0, kw), :]
    qpos = q0 + lax.broadcasted_iota(jnp.int32, (tq, kw), 0)
    kpos = k0 + lax.broadcasted_iota(jnp.int32, (tq, kw), 1)
    valid = jnp.abs(kpos - qpos) <= A_HALF
    lane_head = lax.broadcasted_iota(jnp.int32, (tq, HEAD_BLOCK), 1) // HEAD_DIM
    o = jnp.zeros((tq, HEAD_BLOCK), F32)
    lse = jnp.zeros((tq, HEAD_BLOCK), F32)
    for h in range(HEADS_PER_BLOCK):
        sel = lane_head == h
        qh = jnp.where(sel, q, jnp.zeros_like(q))
        s = lax.dot_general(qh, kwin, (((1,), (1,)), ((), ())), preferred_element_type=F32)
        s = jnp.where(valid, s, NEG_INF)
        m = jnp.max(s, axis=-1, keepdims=True)
        e = jnp.exp(s - m)
        den = jnp.sum(e, axis=-1, keepdims=True)
        p = (e / den).astype(BF16)
        oh = jnp.dot(p, vwin, preferred_element_type=F32)
        o = jnp.where(sel, oh, o)
        lse = jnp.where(sel, m + jnp.log(den), lse)
    o_ref[...] = o.astype(BF16)
    lse_ref[...] = lse


def _dilated_attention(qkv, batch, seq, group):
    _, dil = A_GROUPS[group]
    sub_len = seq // dil
    tq = min(128, sub_len)
    kw = min(256, sub_len)
    n_blk = QKV_E // HEAD_BLOCK
    view = qkv.reshape(batch, sub_len, dil * QKV_E)
    heads = A_WIDTH // HEAD_BLOCK
    o, lse = pl.pallas_call(
        functools.partial(_dilated_body, tq=tq, kw=kw, sub_len=sub_len),
        out_shape=(jax.ShapeDtypeStruct((batch, sub_len, dil * HEAD_BLOCK), BF16),
                   jax.ShapeDtypeStruct((batch, sub_len, dil * HEAD_BLOCK), F32)),
        grid=(batch, dil, sub_len // tq),
        in_specs=[
            pl.BlockSpec((None, tq, HEAD_BLOCK), lambda b, r, i: (b, i, r * n_blk + group)),
            pl.BlockSpec((None, sub_len, HEAD_BLOCK), lambda b, r, i: (b, 0, r * n_blk + heads + group)),
            pl.BlockSpec((None, sub_len, HEAD_BLOCK), lambda b, r, i: (b, 0, r * n_blk + 2 * heads + group)),
        ],
        out_specs=(pl.BlockSpec((None, tq, HEAD_BLOCK), lambda b, r, i: (b, i, r)),
                   pl.BlockSpec((None, tq, HEAD_BLOCK), lambda b, r, i: (b, i, r))),
        compiler_params=_cparams(("arbitrary", "arbitrary", "arbitrary")),
        name=f"dilated_attn_g{group}",
    )(view, view, view)
    return o.reshape(batch * seq, HEAD_BLOCK), lse.reshape(batch * seq, HEAD_BLOCK)


def _ssm_body(u_ref, bmat_ref, apow_ref, ptab_ref, cmat_ref, y_ref, carry_ref, *, tm, reverse):
    @pl.when(pl.program_id(1) == 0)
    def _():
        carry_ref[...] = jnp.zeros_like(carry_ref)

    u = u_ref[...].astype(BF16)
    bu = jnp.dot(u, bmat_ref[...], preferred_element_type=F32)
    row = lax.broadcasted_iota(jnp.int32, (tm, LANES), 0)
    n_steps = int(math.log2(tm))
    y = jnp.zeros((tm, B_WIDTH), F32)
    for j in range(SSM_N // LANES):
        lanes = slice(j * LANES, (j + 1) * LANES)
        hr = bu[:, j * LANES:(j + 1) * LANES]
        hi = bu[:, SSM_N + j * LANES:SSM_N + (j + 1) * LANES]
        for s in range(n_steps):
            k = 1 << s
            ar = apow_ref[s:s + 1, lanes]
            ai = apow_ref[n_steps + s:n_steps + s + 1, lanes]
            if reverse:
                keep = row < tm - k
                sr = jnp.where(keep, pltpu.roll(hr, tm - k, 0), 0.0)
                si = jnp.where(keep, pltpu.roll(hi, tm - k, 0), 0.0)
            else:
                keep = row >= k
                sr = jnp.where(keep, pltpu.roll(hr, k, 0), 0.0)
                si = jnp.where(keep, pltpu.roll(hi, k, 0), 0.0)
            hr, hi = hr + ar * sr - ai * si, hi + ar * si + ai * sr
        cr = carry_ref[0:1, lanes]
        ci = carry_ref[1:2, lanes]
        pr = ptab_ref[:, lanes]
        pi = ptab_ref[:, SSM_N + j * LANES:SSM_N + (j + 1) * LANES]
        hr, hi = hr + pr * cr - pi * ci, hi + pr * ci + pi * cr
        edge = 0 if reverse else tm - 1
        carry_ref[0:1, lanes] = hr[edge:edge + 1, :]
        carry_ref[1:2, lanes] = hi[edge:edge + 1, :]
        y = y + jnp.dot(hr.astype(BF16), cmat_ref[lanes, :], preferred_element_type=F32)
        y = y + jnp.dot(hi.astype(BF16), cmat_ref[SSM_N + j * LANES:SSM_N + (j + 1) * LANES, :],
                        preferred_element_type=F32)
    y_ref[...] = y


def _ssm_scan(u, bmat, apow, ptab, cmat, batch, seq, tm, reverse):
    n_t = seq // tm
    tmap = (lambda b, i: (b, n_t - 1 - i, 0)) if reverse else (lambda b, i: (b, i, 0))
    const = lambda b, i: (0, 0)
    return pl.pallas_call(
        functools.partial(_ssm_body, tm=tm, reverse=reverse),
        out_shape=jax.ShapeDtypeStruct((batch, seq, B_WIDTH), F32),
        grid=(batch, n_t),
        in_specs=[
            pl.BlockSpec((None, tm, B_WIDTH), tmap),
            pl.BlockSpec(bmat.shape, const),
            pl.BlockSpec(apow.shape, const),
            pl.BlockSpec(ptab.shape, const),
            pl.BlockSpec(cmat.shape, const),
        ],
        out_specs=pl.BlockSpec((None, tm, B_WIDTH), tmap),
        scratch_shapes=[pltpu.VMEM((8, SSM_N), F32)],
        compiler_params=_cparams(("arbitrary", "arbitrary")),
        name="ssm_scan_rev" if reverse else "ssm_scan_fwd",
    )(u.reshape(batch, seq, B_WIDTH), bmat, apow, ptab, cmat).reshape(batch * seq, B_WIDTH)


def _ssm_tables(lam_re, lam_im, log_dt, b_re, b_im, c_re, c_im, tm, reverse):
    lr, li = lam_re.astype(F32), lam_im.astype(F32)
    dt = jnp.exp(log_dt.astype(F32))[:, None]
    mag = jnp.exp(lr * dt)
    a_r = mag * jnp.cos(li * dt)
    a_i = mag * jnp.sin(li * dt)
    den = lr * lr + li * li
    f_r = ((a_r - 1.0) * lr + a_i * li) / den
    f_i = (a_i * lr - (a_r - 1.0) * li) / den
    br, bi = b_re.astype(F32), b_im.astype(F32)
    bb_r = f_r[..., None] * br - f_i[..., None] * bi
    bb_i = f_r[..., None] * bi + f_i[..., None] * br
    eye = jnp.eye(B_GROUPS, dtype=F32)

    def in_mat(bb):
        return jnp.einsum('gpc,gh->gchp', bb, eye).reshape(B_WIDTH, SSM_N)

    def out_mat(c):
        return jnp.einsum('gcp,gh->gphc', c.astype(F32), eye).reshape(SSM_N, B_WIDTH)

    bmat = jnp.concatenate([in_mat(bb_r), in_mat(bb_i)], axis=1).astype(BF16)
    cmat = jnp.concatenate([out_mat(c_re), -out_mat(c_im)], axis=0).astype(BF16)

    def cpow(k):
        kk = k.astype(F32)[:, None]
        m = jnp.exp(kk * (lr * dt).reshape(1, SSM_N))
        ang = kk * (li * dt).reshape(1, SSM_N)
        return m * jnp.cos(ang), m * jnp.sin(ang)

    n_steps = int(math.log2(tm))
    sr, si = cpow(2 ** jnp.arange(n_steps))
    apow = jnp.concatenate([sr, si], axis=0)
    t = jnp.arange(tm)
    pr, pi = cpow(tm - t if reverse else t + 1)
    ptab = jnp.concatenate([pr, pi], axis=1)
    return bmat, apow, ptab, cmat


def _gelu_tanh(x):
    return 0.5 * x * (1.0 + jnp.tanh(math.sqrt(2.0 / math.pi) * (x + 0.044715 * (x * x * x))))


def _even_out_body(x_ref, o0_ref, o1_ref, o2_ref, l0_ref, l1_ref, l2_ref, u_ref, yf_ref, yb_ref,
                   d_ref, wglu_ref, wout_ref, out_ref):
    l0, l1, l2 = l0_ref[...], l1_ref[...], l2_ref[...]
    lmax = jnp.maximum(jnp.maximum(l0, l1), l2)
    e0, e1, e2 = jnp.exp(l0 - lmax), jnp.exp(l1 - lmax), jnp.exp(l2 - lmax)
    den = e0 + e1 + e2
    a = ((e0 / den) * o0_ref[...].astype(F32) + (e1 / den) * o1_ref[...].astype(F32)
         + (e2 / den) * o2_ref[...].astype(F32))
    y = u_ref[...] * d_ref[...] + yf_ref[...] + yb_ref[...]
    z = _gelu_tanh(y)
    z = z * jax.nn.sigmoid(jnp.dot(z.astype(BF16), wglu_ref[...], preferred_element_type=F32))
    acc = jnp.dot(a.astype(BF16), wout_ref[0:HEAD_BLOCK, :], preferred_element_type=F32)
    acc = acc + jnp.dot(z.astype(BF16), wout_ref[HEAD_BLOCK:, :], preferred_element_type=F32)
    out_ref[...] = x_ref[...] + acc


def _even_out(x, os_, ls_, u, yf, yb, d_skip, wglu, wout, tm):
    n = x.shape[0]
    row = lambda i: (i, 0)
    const = lambda i: (0, 0)
    narrow = pl.BlockSpec((tm, HEAD_BLOCK), row)
    return pl.pallas_call(
        _even_out_body,
        out_shape=jax.ShapeDtypeStruct((n, D_MODEL), F32),
        grid=(n // tm,),
        in_specs=[pl.BlockSpec((tm, D_MODEL), row)] + [narrow] * 9 + [
            pl.BlockSpec((1, B_WIDTH), const),
            pl.BlockSpec(wglu.shape, const),
            pl.BlockSpec(wout.shape, const),
        ],
        out_specs=pl.BlockSpec((tm, D_MODEL), row),
        compiler_params=_cparams(("arbitrary",)),
        name="even_out",
    )(x, *os_, *ls_, u, yf, yb, d_skip, wglu, wout)


def _ffn_body(x_ref, g_ref, wg_ref, wu_ref, wd_ref, out_ref, *, chunk):
    x = x_ref[...]
    xn = _rms(x, g_ref[...]).astype(BF16)
    acc = jnp.zeros(x.shape, F32)
    for c in range(D_FF // chunk):
        cols = slice(c * chunk, (c + 1) * chunk)
        g = jnp.dot(xn, wg_ref[:, cols], preferred_element_type=F32)
        up = jnp.dot(xn, wu_ref[:, cols], preferred_element_type=F32)
        h = (g * jax.nn.sigmoid(g) * up).astype(BF16)
        acc = acc + jnp.dot(h, wd_ref[cols, :], preferred_element_type=F32)
    out_ref[...] = x + acc


def _ffn(x, ln, wg, wu, wd, tm):
    n = x.shape[0]
    row = lambda i: (i, 0)
    const = lambda i: (0, 0)
    return pl.pallas_call(
        functools.partial(_ffn_body, chunk=256),
        out_shape=jax.ShapeDtypeStruct((n, D_MODEL), F32),
        grid=(n // tm,),
        in_specs=[
            pl.BlockSpec((tm, D_MODEL), row),
            pl.BlockSpec((1, D_MODEL), const),
            pl.BlockSpec(wg.shape, const),
            pl.BlockSpec(wu.shape, const),
            pl.BlockSpec(wd.shape, const),
        ],
        out_specs=pl.BlockSpec((tm, D_MODEL), row),
        compiler_params=_cparams(("arbitrary",)),
        name="ffn_dense",
    )(x, ln, wg, wu, wd)


def _ple_body(x_ref, p_ref, g_ref, wp_ref, wg_ref, out_ref):
    x = x_ref[...]
    xn = _rms(x, g_ref[...]).astype(BF16)
    gate = jax.nn.sigmoid(jnp.dot(xn, wg_ref[...], preferred_element_type=F32))
    emb = jnp.dot(p_ref[...].astype(BF16), wp_ref[...], preferred_element_type=F32)
    out_ref[...] = x + emb * gate


def _ple(x, p, ln, wp, wg, tm):
    n = x.shape[0]
    row = lambda i: (i, 0)
    const = lambda i: (0, 0)
    return pl.pallas_call(
        _ple_body,
        out_shape=jax.ShapeDtypeStruct((n, D_MODEL), F32),
        grid=(n // tm,),
        in_specs=[
            pl.BlockSpec((tm, D_MODEL), row),
            pl.BlockSpec((tm, PLE_DIM), row),
            pl.BlockSpec((1, D_MODEL), const),
            pl.BlockSpec(wp.shape, const),
            pl.BlockSpec(wg.shape, const),
        ],
        out_specs=pl.BlockSpec((tm, D_MODEL), row),
        compiler_params=_cparams(("arbitrary",)),
        name="per_layer_embed",
    )(x, p, ln, wp, wg)


def _gqa_body(qt_ref, k_ref, vt_ref, o_ref, qs_ref, s_ref, m_ref, acc_ref, *, tq, tk, seq):
    n_kv = seq // tk
    for c in range(C_Q_WIDTH // HEAD_BLOCK):
        qs_ref[...] = jnp.zeros(qs_ref.shape, BF16)
        for g in range(HEADS_PER_BLOCK):
            rows = slice(c * HEAD_BLOCK + g * HEAD_DIM, c * HEAD_BLOCK + (g + 1) * HEAD_DIM)
            qs_ref[g * HEAD_DIM:(g + 1) * HEAD_DIM, g * tq:(g + 1) * tq] = qt_ref[rows, :]
        m_ref[...] = jnp.full(m_ref.shape, NEG_INF, F32)
        acc_ref[...] = jnp.zeros(acc_ref.shape, F32)

        def scores(t, slot):
            k0 = pl.multiple_of(t * tk, tk)
            s_ref[slot] = jnp.dot(k_ref[pl.ds(k0, tk), :], qs_ref[...], preferred_element_type=F32)

        def consume(t, slot):
            s = s_ref[slot]
            m_old = m_ref[...]
            m_new = jnp.maximum(m_old, jnp.max(s, axis=0, keepdims=True))
            alpha = jnp.exp2(m_old - m_new)
            p = jnp.exp2(s - m_new).astype(BF16)
            m_ref[...] = m_new
            for g in range(HEADS_PER_BLOCK):
                cols = slice(g * tq, (g + 1) * tq)
                vt = vt_ref[t, g * V_HEAD_ROWS:(g + 1) * V_HEAD_ROWS, :]
                acc_ref[g] = alpha[:, cols] * acc_ref[g] + jnp.dot(vt, p[:, cols], preferred_element_type=F32)

        scores(0, 0)

        def step(u, carry):
            scores(2 * u + 1, 1)
            consume(2 * u, 0)
            scores(jnp.minimum(2 * u + 2, n_kv - 1), 0)
            consume(2 * u + 1, 1)
            return carry

        lax.fori_loop(0, n_kv // 2, step, 0)
        heads = [acc_ref[g, 0:HEAD_DIM, :] / acc_ref[g, HEAD_DIM:HEAD_DIM + 1, :] for g in range(HEADS_PER_BLOCK)]
        o_ref[:, c * HEAD_BLOCK:(c + 1) * HEAD_BLOCK] = jnp.concatenate(heads, axis=0).T.astype(BF16)


def _gqa_attention(qt, k, vt, batch, seq, tq, tk, tm):
    assert tm % tq == 0 and seq % tk == 0
    per_tm = tm // tq
    tiles_per_seq = seq // tm
    return pl.pallas_call(
        functools.partial(_gqa_body, tq=tq, tk=tk, seq=seq),
        out_shape=jax.ShapeDtypeStruct((batch, seq, C_Q_WIDTH), BF16),
        grid=(batch, seq // tq),
        in_specs=[
            pl.BlockSpec((None, C_Q_WIDTH, tq), lambda b, i: (b * tiles_per_seq + i // per_tm, 0, i % per_tm)),
            pl.BlockSpec((None, seq, C_KV_WIDTH), lambda b, i: (b, 0, 0)),
            pl.BlockSpec((seq // tk, HEADS_PER_BLOCK * V_HEAD_ROWS, tk), lambda b, i: (b, 0, 0)),
        ],
        out_specs=pl.BlockSpec((None, tq, C_Q_WIDTH), lambda b, i: (b, i, 0)),
        scratch_shapes=[pltpu.VMEM((HEAD_BLOCK, HEADS_PER_BLOCK * tq), BF16),
                        pltpu.VMEM((2, tk, HEADS_PER_BLOCK * tq), F32),
                        pltpu.VMEM((1, HEADS_PER_BLOCK * tq), F32),
                        pltpu.VMEM((HEADS_PER_BLOCK, V_HEAD_ROWS, tq), F32)],
        compiler_params=_cparams(("arbitrary", "arbitrary")),
        name="gqa_attention",
    )(qt, k.reshape(batch, seq, C_KV_WIDTH), vt).reshape(batch * seq, C_Q_WIDTH)


def _odd_out_body(x_ref, a_ref, wout_ref, g_ref, rw_hi_ref, rw_lo_ref, out_ref, gates_ref):
    x = x_ref[...] + jnp.dot(a_ref[...], wout_ref[...], preferred_element_type=F32)
    out_ref[...] = x
    h = _rms(x, g_ref[...])
    h_hi = h.astype(BF16)
    h_lo = (h - h_hi.astype(F32)).astype(BF16)
    logits = (jnp.dot(h_hi, rw_hi_ref[...], preferred_element_type=F32)
              + jnp.dot(h_hi, rw_lo_ref[...], preferred_element_type=F32)
              + jnp.dot(h_lo, rw_hi_ref[...], preferred_element_type=F32))
    lane = lax.broadcasted_iota(jnp.int32, logits.shape, 1)
    logits = jnp.where(lane < N_EXPERTS, logits, -jnp.inf)
    m1 = jnp.max(logits, axis=-1, keepdims=True)
    i1 = jnp.min(jnp.where(logits == m1, lane, LANES), axis=-1, keepdims=True)
    rest = jnp.where(lane == i1, -jnp.inf, logits)
    m2 = jnp.max(rest, axis=-1, keepdims=True)
    i2 = jnp.min(jnp.where(rest == m2, lane, LANES), axis=-1, keepdims=True)
    e2 = jnp.exp(m2 - m1)
    w1 = 1.0 / (1.0 + e2)
    w2 = e2 / (1.0 + e2)
    gates_ref[...] = jnp.where(lane == i1, w1, 0.0) + jnp.where(lane == i2, w2, 0.0)


def _odd_out(x, attn, wout, ln, rw_hi, rw_lo, tm):
    n = x.shape[0]
    row = lambda i: (i, 0)
    const = lambda i: (0, 0)
    return pl.pallas_call(
        _odd_out_body,
        out_shape=(jax.ShapeDtypeStruct((n, D_MODEL), F32), jax.ShapeDtypeStruct((n, LANES), F32)),
        grid=(n // tm,),
        in_specs=[
            pl.BlockSpec((tm, D_MODEL), row),
            pl.BlockSpec((tm, C_Q_WIDTH), row),
            pl.BlockSpec(wout.shape, const),
            pl.BlockSpec((1, D_MODEL), const),
            pl.BlockSpec(rw_hi.shape, const),
            pl.BlockSpec(rw_lo.shape, const),
        ],
        out_specs=(pl.BlockSpec((tm, D_MODEL), row), pl.BlockSpec((tm, LANES), row)),
        compiler_params=_cparams(("arbitrary",)),
        name="odd_out_router",
    )(x, attn, wout, ln, rw_hi, rw_lo)


def _moe_body(x_ref, gates_ref, g_ref, wg_ref, wu_ref, wd_ref, out_ref, xn_ref, acc_ref):
    e = pl.program_id(1)
    c = pl.program_id(2)

    @pl.when((e == 0) & (c == 0))
    def _():
        xn_ref[...] = _rms(x_ref[...], g_ref[...]).astype(BF16)
        acc_ref[...] = jnp.zeros_like(acc_ref)

    xn = xn_ref[...]
    g = jnp.dot(xn, wg_ref[...], preferred_element_type=F32)
    up = jnp.dot(xn, wu_ref[...], preferred_element_type=F32)
    gates = gates_ref[...]
    lane = lax.broadcasted_iota(jnp.int32, gates.shape, 1)
    ge = jnp.sum(jnp.where(lane == e, gates, 0.0), axis=-1, keepdims=True)
    h = (g * jax.nn.sigmoid(g) * up * ge).astype(BF16)
    acc_ref[...] += jnp.dot(h, wd_ref[...], preferred_element_type=F32)

    @pl.when((e == pl.num_programs(1) - 1) & (c == pl.num_programs(2) - 1))
    def _():
        out_ref[...] = x_ref[...] + acc_ref[...]


def _moe_dense(x, gates, ln, wg, wu, wd, tm, chunk):
    n = x.shape[0]
    row = lambda i, e, c: (i, 0)
    return pl.pallas_call(
        _moe_body,
        out_shape=jax.ShapeDtypeStruct((n, D_MODEL), F32),
        grid=(n // tm, N_EXPERTS, D_EXPERT // chunk),
        in_specs=[
            pl.BlockSpec((tm, D_MODEL), row),
            pl.BlockSpec((tm, LANES), row),
            pl.BlockSpec((1, D_MODEL), lambda i, e, c: (0, 0)),
            pl.BlockSpec((None, D_MODEL, chunk), lambda i, e, c: (e, 0, c)),
            pl.BlockSpec((None, D_MODEL, chunk), lambda i, e, c: (e, 0, c)),
            pl.BlockSpec((None, chunk, D_MODEL), lambda i, e, c: (e, c, 0)),
        ],
        out_specs=pl.BlockSpec((tm, D_MODEL), row),
        scratch_shapes=[pltpu.VMEM((tm, D_MODEL), BF16), pltpu.VMEM((tm, D_MODEL), F32)],
        compiler_params=_cparams(("arbitrary", "arbitrary", "arbitrary")),
        name="moe_dense",
    )(x, gates, ln, wg, wu, wd)


def _rope_tables(pos, dim):
    inv = ROPE_THETA ** (-jnp.arange(0, dim, 2, dtype=F32) / dim)
    ang = pos.astype(F32)[:, None] * inv[None, :]
    ang = jnp.concatenate([ang, ang], axis=-1)
    return jnp.cos(ang), jnp.sin(ang)


def _tile_heads(t):
    return jnp.tile(t, (1, HEADS_PER_BLOCK))


def _even_rope(seq):
    cos, sin = _rope_tables(jnp.arange(seq), HEAD_DIM)
    sign = jnp.where(jnp.arange(HEAD_DIM) < HEAD_DIM // 2, -1.0, 1.0).astype(F32)
    return _tile_heads(cos), _tile_heads(sin * sign)


def _odd_rope(seq):
    t = jnp.arange(seq)
    half = HEAD_DIM // 2
    cr, sr = _rope_tables(t // GRID_W, half)
    cc, sc = _rope_tables(t % GRID_W, half)
    sign = jnp.where(jnp.arange(half) < half // 2, -1.0, 1.0).astype(F32)
    cos = jnp.concatenate([cr, cc], axis=-1)
    sin = jnp.concatenate([sr * sign, sc * sign], axis=-1)
    return _tile_heads(cos), _tile_heads(sin)


def _head_mean_matrix():
    blk = np.arange(HEAD_BLOCK) // HEAD_DIM
    return jnp.asarray((blk[:, None] == blk[None, :]).astype(np.float32) / HEAD_DIM, dtype=BF16)


def _q_head_permutation():
    n_blocks = C_Q_WIDTH // HEAD_BLOCK
    heads = [HEADS_PER_BLOCK * g + c for c in range(n_blocks) for g in range(HEADS_PER_BLOCK)]
    return np.concatenate([np.arange(h * HEAD_DIM, (h + 1) * HEAD_DIM) for h in heads])


def _row(v):
    return v.astype(F32).reshape(1, -1)


def _even_layer(x, batch, seq, w):
    tm = 512
    cos, sin = _even_rope(seq)
    qkv, u = _proj_even(x, w['ln_mix'], w['w_in'], w['bd'], w['qn'], w['kn'], cos, sin, seq, tm)
    outs, lses = [], []
    for g in range(len(A_GROUPS)):
        o, lse = _dilated_attention(qkv, batch, seq, g)
        outs.append(o)
        lses.append(lse)
    ssm_tm = 128
    ys = []
    for direction in range(2):
        tabs = _ssm_tables(*(w[k][direction] for k in ('lam_re', 'lam_im', 'log_dt', 'b_re', 'b_im', 'c_re', 'c_im')),
                           ssm_tm, direction == 1)
        ys.append(_ssm_scan(u, *tabs, batch, seq, ssm_tm, direction == 1))
    x = _even_out(x, outs, lses, u, ys[0], ys[1], w['d_skip'], w['w_glu'], w['w_out'], tm)
    return _ffn(x, w['ln_ffn'], w['ffn_gate'], w['ffn_up'], w['ffn_down'], 256)


def _odd_layer(x, batch, seq, w):
    tm = 512
    cos, sin = _odd_rope(seq)
    qt, k, vt = _proj_odd(x, w['ln_mix'], w['w_in'], w['bd'], w['qn'], w['kn'], cos, sin, seq, tm)
    attn = _gqa_attention(qt, k, vt, batch, seq, 256, tm, tm)
    x, gates = _odd_out(x, attn, w['w_out'], w['ln_ffn'], w['router_hi'], w['router_lo'], tm)
    return _moe_dense(x, gates, w['ln_ffn'], w['moe_gate'], w['moe_up'], w['moe_down'], 1024, 512)


def _prepare_even(j, ln_mix_e, w_in_e, a_qnorm, a_knorm, lam_re, lam_im, log_dt, b_re, b_im, c_re, c_im,
                  ssm_d, ssm_w_glu, w_out_e, ln_ffn_e, ffn_w_gate, ffn_w_up, ffn_w_down):
    return dict(
        ln_mix=_row(ln_mix_e[j]), w_in=w_in_e[j].astype(BF16), bd=_head_mean_matrix(),
        qn=_tile_heads(_row(a_qnorm[j])), kn=_tile_heads(_row(a_knorm[j])),
        lam_re=lam_re[j], lam_im=lam_im[j], log_dt=log_dt[j], b_re=b_re[j], b_im=b_im[j],
        c_re=c_re[j], c_im=c_im[j], d_skip=_row(ssm_d[j]), w_glu=ssm_w_glu[j].astype(BF16),
        w_out=w_out_e[j].astype(BF16), ln_ffn=_row(ln_ffn_e[j]), ffn_gate=ffn_w_gate[j].astype(BF16),
        ffn_up=ffn_w_up[j].astype(BF16), ffn_down=ffn_w_down[j].astype(BF16))


def _prepare_odd(j, ln_mix_o, w_in_o, c_qnorm, c_knorm, w_out_o, ln_ffn_o, router_w, moe_w_gate, moe_w_up,
                 moe_w_down):
    perm = _q_head_permutation()
    w_in = w_in_o[j]
    w_in = jnp.concatenate([w_in[:, :C_Q_WIDTH][:, perm], w_in[:, C_Q_WIDTH:]], axis=1).astype(BF16)
    rw = jnp.pad(router_w[j].astype(F32), ((0, 0), (0, LANES - N_EXPERTS)))
    rw_hi = rw.astype(BF16)
    rw_lo = (rw - rw_hi.astype(F32)).astype(BF16)
    return dict(
        ln_mix=_row(ln_mix_o[j]), w_in=w_in, bd=_head_mean_matrix(),
        qn=_tile_heads(_row(c_qnorm[j])), kn=_tile_heads(_row(c_knorm[j])),
        w_out=w_out_o[j][perm, :].astype(BF16), ln_ffn=_row(ln_ffn_o[j]), router_hi=rw_hi, router_lo=rw_lo,
        moe_gate=moe_w_gate[j].astype(BF16), moe_up=moe_w_up[j].astype(BF16), moe_down=moe_w_down[j].astype(BF16))


def _trunk(x, p, layers, ple):
    batch, seq, _ = x.shape
    x = x.reshape(batch * seq, D_MODEL)
    for i, w in enumerate(layers):
        x = (_even_layer if i % 2 == 0 else _odd_layer)(x, batch, seq, w)
        x = _ple(x, p[i].reshape(batch * seq, PLE_DIM), *ple[i], 512)
    return x.reshape(batch, seq, D_MODEL)


def kernel(x_prompt, x_sample, p_prompt, p_sample, ln_mix_e, w_in_e, a_qnorm, a_knorm, ssm_lam_re, ssm_lam_im, ssm_log_dt, ssm_b_re, ssm_b_im, ssm_c_re, ssm_c_im, ssm_d, ssm_w_glu, w_out_e, ln_ffn_e, ffn_w_gate, ffn_w_up, ffn_w_down, ln_mix_o, w_in_o, c_qnorm, c_knorm, w_out_o, ln_ffn_o, router_w, moe_w_gate, moe_w_up, moe_w_down, ple_ln, ple_w_proj, ple_w_gate):
    depth = p_prompt.shape[0]
    layers = []
    for i in range(depth):
        j = i // 2
        if i % 2 == 0:
            layers.append(_prepare_even(j, ln_mix_e, w_in_e, a_qnorm, a_knorm, ssm_lam_re, ssm_lam_im, ssm_log_dt,
                                        ssm_b_re, ssm_b_im, ssm_c_re, ssm_c_im, ssm_d, ssm_w_glu, w_out_e, ln_ffn_e,
                                        ffn_w_gate, ffn_w_up, ffn_w_down))
        else:
            layers.append(_prepare_odd(j, ln_mix_o, w_in_o, c_qnorm, c_knorm, w_out_o, ln_ffn_o, router_w,
                                       moe_w_gate, moe_w_up, moe_w_down))
    ple = [(_row(ple_ln[i]), ple_w_proj[i].astype(BF16), ple_w_gate[i].astype(BF16)) for i in range(depth)]
    y_prompt = _trunk(x_prompt, p_prompt, layers, ple)
    y_sample = _trunk(x_sample, p_sample, layers, ple)
    return (y_prompt, y_sample)
```

```python
import functools
import math

import jax
import jax.numpy as jnp
import numpy as np
from jax import lax
from jax.experimental import pallas as pl
from jax.experimental.pallas import tpu as pltpu

F32 = jnp.float32
BF16 = jnp.bfloat16

D_MODEL = 1024
HEAD_DIM = 64
EPS = 1e-6
ROPE_THETA = 10000.0
NEG_INF = -1e30
LOG2_E = math.log2(math.e)
LANES = 128
HEAD_BLOCK = 256
HEADS_PER_BLOCK = HEAD_BLOCK // HEAD_DIM
A_GROUPS = ((128, 1), (512, 4), (2048, 16))
A_HALF = 64
A_WIDTH = 768
B_WIDTH = 256
B_GROUP_CH = 16
B_GROUPS = 16
B_STATE = 64
SSM_N = B_GROUPS * B_STATE
QKV_E = 3 * A_WIDTH
D_FF = 2816
GRID_W = 64
C_Q_WIDTH = 1024
C_KV_WIDTH = 256
V_ONES_ROWS = 16
V_HEAD_ROWS = HEAD_DIM + V_ONES_ROWS
N_EXPERTS = 8
TOP_K = 2
D_EXPERT = 3584
MOE_TOKEN_TILE = 512
MOE_ROW_TILE = 512
MOE_FF_CHUNK = 512
PLE_DIM = 256

VMEM_LIMIT = 56 * 1024 * 1024


def _cparams(sem):
    return pltpu.CompilerParams(dimension_semantics=sem, vmem_limit_bytes=VMEM_LIMIT)


def _rms(x, g):
    ms = jnp.mean(x * x, axis=-1, keepdims=True)
    return x * lax.rsqrt(ms + EPS) * g


def _headnorm_rope(y, bd, gain, cos, sin_signed, half):
    ss = jnp.dot((y * y).astype(BF16), bd, preferred_element_type=F32)
    yn = y * lax.rsqrt(ss + EPS) * gain
    width = yn.shape[1]
    lane = lax.broadcasted_iota(jnp.int32, yn.shape, 1)
    up = pltpu.roll(yn, width - half, 1)
    dn = pltpu.roll(yn, half, 1)
    rot = jnp.where((lane & (2 * half - 1)) < half, up, dn)
    return yn * cos + rot * sin_signed


def _proj_even_body(x_ref, g_ref, w_ref, bd_ref, qn_ref, kn_ref, cos_ref, sin_ref, qkv_ref, u_ref):
    xn = _rms(x_ref[...], g_ref[...]).astype(BF16)
    n_head_blocks = A_WIDTH // HEAD_BLOCK
    for c in range(QKV_E // HEAD_BLOCK + 1):
        y = jnp.dot(xn, w_ref[:, c * HEAD_BLOCK:(c + 1) * HEAD_BLOCK], preferred_element_type=F32)
        if c < 2 * n_head_blocks:
            gain = qn_ref[...] if c < n_head_blocks else kn_ref[...]
            y = _headnorm_rope(y, bd_ref[...], gain, cos_ref[...], sin_ref[...], HEAD_DIM // 2)
            if c < n_head_blocks:
                y = y * (HEAD_DIM ** -0.5)
        if c < QKV_E // HEAD_BLOCK:
            qkv_ref[:, c * HEAD_BLOCK:(c + 1) * HEAD_BLOCK] = y.astype(BF16)
        else:
            u_ref[...] = y


def _proj_even(x, ln, w, bd, qn, kn, cos, sin, seq, tm):
    n = x.shape[0]
    n_seq_tiles = seq // tm
    const = lambda i: (0, 0)
    return pl.pallas_call(
        _proj_even_body,
        out_shape=(jax.ShapeDtypeStruct((n, QKV_E), BF16), jax.ShapeDtypeStruct((n, B_WIDTH), F32)),
        grid=(n // tm,),
        in_specs=[
            pl.BlockSpec((tm, D_MODEL), lambda i: (i, 0)),
            pl.BlockSpec((1, D_MODEL), const),
            pl.BlockSpec(w.shape, const),
            pl.BlockSpec(bd.shape, const),
            pl.BlockSpec((1, HEAD_BLOCK), const),
            pl.BlockSpec((1, HEAD_BLOCK), const),
            pl.BlockSpec((tm, HEAD_BLOCK), lambda i: (i % n_seq_tiles, 0)),
            pl.BlockSpec((tm, HEAD_BLOCK), lambda i: (i % n_seq_tiles, 0)),
        ],
        out_specs=(pl.BlockSpec((tm, QKV_E), lambda i: (i, 0)), pl.BlockSpec((tm, B_WIDTH), lambda i: (i, 0))),
        compiler_params=_cparams(("arbitrary",)),
        name="proj_even",
    )(x, ln, w, bd, qn, kn, cos, sin)


def _proj_odd_body(x_ref, g_ref, w_ref, bd_ref, qn_ref, kn_ref, cos_ref, sin_ref, qt_ref, k_ref, vt_ref):
    xn = _rms(x_ref[...], g_ref[...]).astype(BF16)
    tm = xn.shape[0]
    n_q_blocks = C_Q_WIDTH // HEAD_BLOCK
    for c in range(n_q_blocks + 2):
        y = jnp.dot(xn, w_ref[:, c * HEAD_BLOCK:(c + 1) * HEAD_BLOCK], preferred_element_type=F32)
        if c <= n_q_blocks:
            gain = qn_ref[...] if c < n_q_blocks else kn_ref[...]
            y = _headnorm_rope(y, bd_ref[...], gain, cos_ref[...], sin_ref[...], HEAD_DIM // 4)
        if c < n_q_blocks:
            qt_ref[c * HEAD_BLOCK:(c + 1) * HEAD_BLOCK, :] = (y * (HEAD_DIM ** -0.5 * LOG2_E)).T.astype(BF16)
        elif c == n_q_blocks:
            k_ref[...] = y.astype(BF16)
        else:
            yt = y.T.astype(BF16)
            ones = jnp.ones((V_ONES_ROWS, tm), BF16)
            for g in range(HEADS_PER_BLOCK):
                vt_ref[g * V_HEAD_ROWS:g * V_HEAD_ROWS + HEAD_DIM, :] = yt[g * HEAD_DIM:(g + 1) * HEAD_DIM, :]
                vt_ref[g * V_HEAD_ROWS + HEAD_DIM:(g + 1) * V_HEAD_ROWS, :] = ones


def _proj_odd(x, ln, w, bd, qn, kn, cos, sin, seq, tm):
    n = x.shape[0]
    n_seq_tiles = seq // tm
    const = lambda i: (0, 0)
    return pl.pallas_call(
        _proj_odd_body,
        out_shape=(jax.ShapeDtypeStruct((n // tm, C_Q_WIDTH, tm), BF16),
                   jax.ShapeDtypeStruct((n, C_KV_WIDTH), BF16),
                   jax.ShapeDtypeStruct((n // tm, HEADS_PER_BLOCK * V_HEAD_ROWS, tm), BF16)),
        grid=(n // tm,),
        in_specs=[
            pl.BlockSpec((tm, D_MODEL), lambda i: (i, 0)),
            pl.BlockSpec((1, D_MODEL), const),
            pl.BlockSpec(w.shape, const),
            pl.BlockSpec(bd.shape, const),
            pl.BlockSpec((1, HEAD_BLOCK), const),
            pl.BlockSpec((1, HEAD_BLOCK), const),
            pl.BlockSpec((tm, HEAD_BLOCK), lambda i: (i % n_seq_tiles, 0)),
            pl.BlockSpec((tm, HEAD_BLOCK), lambda i: (i % n_seq_tiles, 0)),
        ],
        out_specs=(pl.BlockSpec((None, C_Q_WIDTH, tm), lambda i: (i, 0, 0)),
                   pl.BlockSpec((tm, C_KV_WIDTH), lambda i: (i, 0)),
                   pl.BlockSpec((None, HEADS_PER_BLOCK * V_HEAD_ROWS, tm), lambda i: (i, 0, 0))),
        compiler_params=_cparams(("arbitrary",)),
        name="proj_odd",
    )(x, ln, w, bd, qn, kn, cos, sin)


def _dilated_body(q_ref, k_ref, v_ref, o_ref, lse_ref, *, tq, kw, sub_len):
    i = pl.program_id(2)
    q0 = i * tq
    k0 = pl.multiple_of(jnp.clip(q0 - A_HALF, 0, sub_len - kw), A_HALF)
    q = q_ref[...]
    kwin = k_ref[pl.ds(k0, kw), :]
    vwin = v_ref[pl.ds(k0, kw), :]
    qpos = q0 + lax.broadcasted_iota(jnp.int32, (tq, kw), 0)
    kpos = k0 + lax.broadcasted_iota(jnp.int32, (tq, kw), 1)
    valid = jnp.abs(kpos - qpos) <= A_HALF
    lane_head = lax.broadcasted_iota(jnp.int32, (tq, HEAD_BLOCK), 1) // HEAD_DIM
    o = jnp.zeros((tq, HEAD_BLOCK), F32)
    lse = jnp.zeros((tq, HEAD_BLOCK), F32)
    for h in range(HEADS_PER_BLOCK):
        sel = lane_head == h
        qh = jnp.where(sel, q, jnp.zeros_like(q))
        s = lax.dot_general(qh, kwin, (((1,), (1,)), ((), ())), preferred_element_type=F32)
        s = jnp.where(valid, s, NEG_INF)
        m = jnp.max(s, axis=-1, keepdims=True)
        e = jnp.exp(s - m)
        den = jnp.sum(e, axis=-1, keepdims=True)
        p = (e / den).astype(BF16)
        oh = jnp.dot(p, vwin, preferred_element_type=F32)
        o = jnp.where(sel, oh, o)
        lse = jnp.where(sel, m + jnp.log(den), lse)
    o_ref[...] = o.astype(BF16)
    lse_ref[...] = lse


def _dilated_attention(qkv, batch, seq, group):
    _, dil = A_GROUPS[group]
    sub_len = seq // dil
    tq = min(128, sub_len)
    kw = min(256, sub_len)
    n_blk = QKV_E // HEAD_BLOCK
    view = qkv.reshape(batch, sub_len, dil * QKV_E)
    heads = A_WIDTH // HEAD_BLOCK
    o, lse = pl.pallas_call(
        functools.partial(_dilated_body, tq=tq, kw=kw, sub_len=sub_len),
        out_shape=(jax.ShapeDtypeStruct((batch, sub_len, dil * HEAD_BLOCK), BF16),
                   jax.ShapeDtypeStruct((batch, sub_len, dil * HEAD_BLOCK), F32)),
        grid=(batch, dil, sub_len // tq),
        in_specs=[
            pl.BlockSpec((None, tq, HEAD_BLOCK), lambda b, r, i: (b, i, r * n_blk + group)),
            pl.BlockSpec((None, sub_len, HEAD_BLOCK), lambda b, r, i: (b, 0, r * n_blk + heads + group)),
            pl.BlockSpec((None, sub_len, HEAD_BLOCK), lambda b, r, i: (b, 0, r * n_blk + 2 * heads + group)),
        ],
        out_specs=(pl.BlockSpec((None, tq, HEAD_BLOCK), lambda b, r, i: (b, i, r)),
                   pl.BlockSpec((None, tq, HEAD_BLOCK), lambda b, r, i: (b, i, r))),
        compiler_params=_cparams(("arbitrary", "arbitrary", "arbitrary")),
        name=f"dilated_attn_g{group}",
    )(view, view, view)
    return o.reshape(batch * seq, HEAD_BLOCK), lse.reshape(batch * seq, HEAD_BLOCK)


def _ssm_body(u_ref, bmat_ref, apow_ref, ptab_ref, cmat_ref, y_ref, carry_ref, *, tm, reverse):
    @pl.when(pl.program_id(1) == 0)
    def _():
        carry_ref[...] = jnp.zeros_like(carry_ref)

    u = u_ref[...].astype(BF16)
    bu = jnp.dot(u, bmat_ref[...], preferred_element_type=F32)
    row = lax.broadcasted_iota(jnp.int32, (tm, LANES), 0)
    n_steps = int(math.log2(tm))
    y = jnp.zeros((tm, B_WIDTH), F32)
    for j in range(SSM_N // LANES):
        lanes = slice(j * LANES, (j + 1) * LANES)
        hr = bu[:, j * LANES:(j + 1) * LANES]
        hi = bu[:, SSM_N + j * LANES:SSM_N + (j + 1) * LANES]
        for s in range(n_steps):
            k = 1 << s
            ar = apow_ref[s:s + 1, lanes]
            ai = apow_ref[n_steps + s:n_steps + s + 1, lanes]
            if reverse:
                keep = row < tm - k
                sr = jnp.where(keep, pltpu.roll(hr, tm - k, 0), 0.0)
                si = jnp.where(keep, pltpu.roll(hi, tm - k, 0), 0.0)
            else:
                keep = row >= k
                sr = jnp.where(keep, pltpu.roll(hr, k, 0), 0.0)
                si = jnp.where(keep, pltpu.roll(hi, k, 0), 0.0)
            hr, hi = hr + ar * sr - ai * si, hi + ar * si + ai * sr
        cr = carry_ref[0:1, lanes]
        ci = carry_ref[1:2, lanes]
        pr = ptab_ref[:, lanes]
        pi = ptab_ref[:, SSM_N + j * LANES:SSM_N + (j + 1) * LANES]
        hr, hi = hr + pr * cr - pi * ci, hi + pr * ci + pi * cr
        edge = 0 if reverse else tm - 1
        carry_ref[0:1, lanes] = hr[edge:edge + 1, :]
        carry_ref[1:2, lanes] = hi[edge:edge + 1, :]
        y = y + jnp.dot(hr.astype(BF16), cmat_ref[lanes, :], preferred_element_type=F32)
        y = y + jnp.dot(hi.astype(BF16), cmat_ref[SSM_N + j * LANES:SSM_N + (j + 1) * LANES, :],
                        preferred_element_type=F32)
    y_ref[...] = y


def _ssm_scan(u, bmat, apow, ptab, cmat, batch, seq, tm, reverse):
    n_t = seq // tm
    tmap = (lambda b, i: (b, n_t - 1 - i, 0)) if reverse else (lambda b, i: (b, i, 0))
    const = lambda b, i: (0, 0)
    return pl.pallas_call(
        functools.partial(_ssm_body, tm=tm, reverse=reverse),
        out_shape=jax.ShapeDtypeStruct((batch, seq, B_WIDTH), F32),
        grid=(batch, n_t),
        in_specs=[
            pl.BlockSpec((None, tm, B_WIDTH), tmap),
            pl.BlockSpec(bmat.shape, const),
            pl.BlockSpec(apow.shape, const),
            pl.BlockSpec(ptab.shape, const),
            pl.BlockSpec(cmat.shape, const),
        ],
        out_specs=pl.BlockSpec((None, tm, B_WIDTH), tmap),
        scratch_shapes=[pltpu.VMEM((8, SSM_N), F32)],
        compiler_params=_cparams(("arbitrary", "arbitrary")),
        name="ssm_scan_rev" if reverse else "ssm_scan_fwd",
    )(u.reshape(batch, seq, B_WIDTH), bmat, apow, ptab, cmat).reshape(batch * seq, B_WIDTH)


def _ssm_tables(lam_re, lam_im, log_dt, b_re, b_im, c_re, c_im, tm, reverse):
    lr, li = lam_re.astype(F32), lam_im.astype(F32)
    dt = jnp.exp(log_dt.astype(F32))[:, None]
    mag = jnp.exp(lr * dt)
    a_r = mag * jnp.cos(li * dt)
    a_i = mag * jnp.sin(li * dt)
    den = lr * lr + li * li
    f_r = ((a_r - 1.0) * lr + a_i * li) / den
    f_i = (a_i * lr - (a_r - 1.0) * li) / den
    br, bi = b_re.astype(F32), b_im.astype(F32)
    bb_r = f_r[..., None] * br - f_i[..., None] * bi
    bb_i = f_r[..., None] * bi + f_i[..., None] * br
    eye = jnp.eye(B_GROUPS, dtype=F32)

    def in_mat(bb):
        return jnp.einsum('gpc,gh->gchp', bb, eye).reshape(B_WIDTH, SSM_N)

    def out_mat(c):
        return jnp.einsum('gcp,gh->gphc', c.astype(F32), eye).reshape(SSM_N, B_WIDTH)

    bmat = jnp.concatenate([in_mat(bb_r), in_mat(bb_i)], axis=1).astype(BF16)
    cmat = jnp.concatenate([out_mat(c_re), -out_mat(c_im)], axis=0).astype(BF16)

    def cpow(k):
        kk = k.astype(F32)[:, None]
        m = jnp.exp(kk * (lr * dt).reshape(1, SSM_N))
        ang = kk * (li * dt).reshape(1, SSM_N)
        return m * jnp.cos(ang), m * jnp.sin(ang)

    n_steps = int(math.log2(tm))
    sr, si = cpow(2 ** jnp.arange(n_steps))
    apow = jnp.concatenate([sr, si], axis=0)
    t = jnp.arange(tm)
    pr, pi = cpow(tm - t if reverse else t + 1)
    ptab = jnp.concatenate([pr, pi], axis=1)
    return bmat, apow, ptab, cmat


def _gelu_tanh(x):
    return 0.5 * x * (1.0 + jnp.tanh(math.sqrt(2.0 / math.pi) * (x + 0.044715 * (x * x * x))))


def _even_out_body(x_ref, o0_ref, o1_ref, o2_ref, l0_ref, l1_ref, l2_ref, u_ref, yf_ref, yb_ref,
                   d_ref, wglu_ref, wout_ref, out_ref):
    l0, l1, l2 = l0_ref[...], l1_ref[...], l2_ref[...]
    lmax = jnp.maximum(jnp.maximum(l0, l1), l2)
    e0, e1, e2 = jnp.exp(l0 - lmax), jnp.exp(l1 - lmax), jnp.exp(l2 - lmax)
    den = e0 + e1 + e2
    a = ((e0 / den) * o0_ref[...].astype(F32) + (e1 / den) * o1_ref[...].astype(F32)
         + (e2 / den) * o2_ref[...].astype(F32))
    y = u_ref[...] * d_ref[...] + yf_ref[...] + yb_ref[...]
    z = _gelu_tanh(y)
    z = z * jax.nn.sigmoid(jnp.dot(z.astype(BF16), wglu_ref[...], preferred_element_type=F32))
    acc = jnp.dot(a.astype(BF16), wout_ref[0:HEAD_BLOCK, :], preferred_element_type=F32)
    acc = acc + jnp.dot(z.astype(BF16), wout_ref[HEAD_BLOCK:, :], preferred_element_type=F32)
    out_ref[...] = x_ref[...] + acc


def _even_out(x, os_, ls_, u, yf, yb, d_skip, wglu, wout, tm):
    n = x.shape[0]
    row = lambda i: (i, 0)
    const = lambda i: (0, 0)
    narrow = pl.BlockSpec((tm, HEAD_BLOCK), row)
    return pl.pallas_call(
        _even_out_body,
        out_shape=jax.ShapeDtypeStruct((n, D_MODEL), F32),
        grid=(n // tm,),
        in_specs=[pl.BlockSpec((tm, D_MODEL), row)] + [narrow] * 9 + [
            pl.BlockSpec((1, B_WIDTH), const),
            pl.BlockSpec(wglu.shape, const),
            pl.BlockSpec(wout.shape, const),
        ],
        out_specs=pl.BlockSpec((tm, D_MODEL), row),
        compiler_params=_cparams(("arbitrary",)),
        name="even_out",
    )(x, *os_, *ls_, u, yf, yb, d_skip, wglu, wout)


def _ffn_body(x_ref, g_ref, wg_ref, wu_ref, wd_ref, out_ref, *, chunk):
    x = x_ref[...]
    xn = _rms(x, g_ref[...]).astype(BF16)
    acc = jnp.zeros(x.shape, F32)
    for c in range(D_FF // chunk):
        cols = slice(c * chunk, (c + 1) * chunk)
        g = jnp.dot(xn, wg_ref[:, cols], preferred_element_type=F32)
        up = jnp.dot(xn, wu_ref[:, cols], preferred_element_type=F32)
        h = (g * jax.nn.sigmoid(g) * up).astype(BF16)
        acc = acc + jnp.dot(h, wd_ref[cols, :], preferred_element_type=F32)
    out_ref[...] = x + acc


def _ffn(x, ln, wg, wu, wd, tm):
    n = x.shape[0]
    row = lambda i: (i, 0)
    const = lambda i: (0, 0)
    return pl.pallas_call(
        functools.partial(_ffn_body, chunk=256),
        out_shape=jax.ShapeDtypeStruct((n, D_MODEL), F32),
        grid=(n // tm,),
        in_specs=[
            pl.BlockSpec((tm, D_MODEL), row),
            pl.BlockSpec((1, D_MODEL), const),
            pl.BlockSpec(wg.shape, const),
            pl.BlockSpec(wu.shape, const),
            pl.BlockSpec(wd.shape, const),
        ],
        out_specs=pl.BlockSpec((tm, D_MODEL), row),
        compiler_params=_cparams(("arbitrary",)),
        name="ffn_dense",
    )(x, ln, wg, wu, wd)


def _ple_body(x_ref, p_ref, g_ref, wp_ref, wg_ref, out_ref):
    x = x_ref[...]
    xn = _rms(x, g_ref[...]).astype(BF16)
    gate = jax.nn.sigmoid(jnp.dot(xn, wg_ref[...], preferred_element_type=F32))
    emb = jnp.dot(p_ref[...].astype(BF16), wp_ref[...], preferred_element_type=F32)
    out_ref[...] = x + emb * gate


def _ple(x, p, ln, wp, wg, tm):
    n = x.shape[0]
    row = lambda i: (i, 0)
    const = lambda i: (0, 0)
    return pl.pallas_call(
        _ple_body,
        out_shape=jax.ShapeDtypeStruct((n, D_MODEL), F32),
        grid=(n // tm,),
        in_specs=[
            pl.BlockSpec((tm, D_MODEL), row),
            pl.BlockSpec((tm, PLE_DIM), row),
            pl.BlockSpec((1, D_MODEL), const),
            pl.BlockSpec(wp.shape, const),
            pl.BlockSpec(wg.shape, const),
        ],
        out_specs=pl.BlockSpec((tm, D_MODEL), row),
        compiler_params=_cparams(("arbitrary",)),
        name="per_layer_embed",
    )(x, p, ln, wp, wg)


def _gqa_body(qt_ref, k_ref, vt_ref, o_ref, qs_ref, s_ref, m_ref, acc_ref, *, tq, tk, seq):
    n_kv = seq // tk
    for c in range(C_Q_WIDTH // HEAD_BLOCK):
        qs_ref[...] = jnp.zeros(qs_ref.shape, BF16)
        for g in range(HEADS_PER_BLOCK):
            rows = slice(c * HEAD_BLOCK + g * HEAD_DIM, c * HEAD_BLOCK + (g + 1) * HEAD_DIM)
            qs_ref[g * HEAD_DIM:(g + 1) * HEAD_DIM, g * tq:(g + 1) * tq] = qt_ref[rows, :]
        m_ref[...] = jnp.full(m_ref.shape, NEG_INF, F32)
        acc_ref[...] = jnp.zeros(acc_ref.shape, F32)

        def scores(t, slot):
            k0 = pl.multiple_of(t * tk, tk)
            s_ref[slot] = jnp.dot(k_ref[pl.ds(k0, tk), :], qs_ref[...], preferred_element_type=F32)

        def consume(t, slot):
            s = s_ref[slot]
            m_old = m_ref[...]
            m_new = jnp.maximum(m_old, jnp.max(s, axis=0, keepdims=True))
            alpha = jnp.exp2(m_old - m_new)
            p = jnp.exp2(s - m_new).astype(BF16)
            m_ref[...] = m_new
            for g in range(HEADS_PER_BLOCK):
                cols = slice(g * tq, (g + 1) * tq)
                vt = vt_ref[t, g * V_HEAD_ROWS:(g + 1) * V_HEAD_ROWS, :]
                acc_ref[g] = alpha[:, cols] * acc_ref[g] + jnp.dot(vt, p[:, cols], preferred_element_type=F32)

        scores(0, 0)

        def step(u, carry):
            scores(2 * u + 1, 1)
            consume(2 * u, 0)
            scores(jnp.minimum(2 * u + 2, n_kv - 1), 0)
            consume(2 * u + 1, 1)
            return carry

        lax.fori_loop(0, n_kv // 2, step, 0)
        heads = [acc_ref[g, 0:HEAD_DIM, :] / acc_ref[g, HEAD_DIM:HEAD_DIM + 1, :] for g in range(HEADS_PER_BLOCK)]
        o_ref[:, c * HEAD_BLOCK:(c + 1) * HEAD_BLOCK] = jnp.concatenate(heads, axis=0).T.astype(BF16)


def _gqa_attention(qt, k, vt, batch, seq, tq, tk, tm):
    assert tm % tq == 0 and seq % tk == 0
    per_tm = tm // tq
    tiles_per_seq = seq // tm
    return pl.pallas_call(
        functools.partial(_gqa_body, tq=tq, tk=tk, seq=seq),
        out_shape=jax.ShapeDtypeStruct((batch, seq, C_Q_WIDTH), BF16),
        grid=(batch, seq // tq),
        in_specs=[
            pl.BlockSpec((None, C_Q_WIDTH, tq), lambda b, i: (b * tiles_per_seq + i // per_tm, 0, i % per_tm)),
            pl.BlockSpec((None, seq, C_KV_WIDTH), lambda b, i: (b, 0, 0)),
            pl.BlockSpec((seq // tk, HEADS_PER_BLOCK * V_HEAD_ROWS, tk), lambda b, i: (b, 0, 0)),
        ],
        out_specs=pl.BlockSpec((None, tq, C_Q_WIDTH), lambda b, i: (b, i, 0)),
        scratch_shapes=[pltpu.VMEM((HEAD_BLOCK, HEADS_PER_BLOCK * tq), BF16),
                        pltpu.VMEM((2, tk, HEADS_PER_BLOCK * tq), F32),
                        pltpu.VMEM((1, HEADS_PER_BLOCK * tq), F32),
                        pltpu.VMEM((HEADS_PER_BLOCK, V_HEAD_ROWS, tq), F32)],
        compiler_params=_cparams(("arbitrary", "arbitrary")),
        name="gqa_attention",
    )(qt, k.reshape(batch, seq, C_KV_WIDTH), vt).reshape(batch * seq, C_Q_WIDTH)


def _odd_out_body(x_ref, a_ref, wout_ref, g_ref, rw_hi_ref, rw_lo_ref, out_ref, xn_ref, route_ref, route_t_ref):
    x = x_ref[...] + jnp.dot(a_ref[...], wout_ref[...], preferred_element_type=F32)
    out_ref[...] = x
    h = _rms(x, g_ref[...])
    h_hi = h.astype(BF16)
    h_lo = (h - h_hi.astype(F32)).astype(BF16)
    logits = (jnp.dot(h_hi, rw_hi_ref[...], preferred_element_type=F32)
              + jnp.dot(h_hi, rw_lo_ref[...], preferred_element_type=F32)
              + jnp.dot(h_lo, rw_hi_ref[...], preferred_element_type=F32))
    lane = lax.broadcasted_iota(jnp.int32, logits.shape, 1)
    logits = jnp.where(lane < N_EXPERTS, logits, -jnp.inf)
    m1 = jnp.max(logits, axis=-1, keepdims=True)
    i1 = jnp.min(jnp.where(logits == m1, lane, LANES), axis=-1, keepdims=True)
    rest = jnp.where(lane == i1, -jnp.inf, logits)
    m2 = jnp.max(rest, axis=-1, keepdims=True)
    i2 = jnp.min(jnp.where(rest == m2, lane, LANES), axis=-1, keepdims=True)
    e2 = jnp.exp(m2 - m1)
    w1 = 1.0 / (1.0 + e2)
    w2 = e2 / (1.0 + e2)
    rec = jnp.where(lane == 0, i1.astype(F32),
                    jnp.where(lane == 1, i2.astype(F32), jnp.where(lane == 2, w1, jnp.where(lane == 3, w2, 0.0))))
    route_ref[...] = rec
    route_t_ref[...] = rec.T[0:8, :]
    xn_ref[...] = _pack_pairs(h_hi[:, :D_MODEL // 2].astype(F32), h_hi[:, D_MODEL // 2:].astype(F32))


def _odd_out(x, attn, wout, ln, rw_hi, rw_lo, tm):
    n = x.shape[0]
    row = lambda i: (i, 0)
    const = lambda i: (0, 0)
    return pl.pallas_call(
        _odd_out_body,
        out_shape=(jax.ShapeDtypeStruct((n, D_MODEL), F32), jax.ShapeDtypeStruct((n, D_MODEL // 2), jnp.uint32),
                   jax.ShapeDtypeStruct((n, LANES), F32), jax.ShapeDtypeStruct((8, n), F32)),
        grid=(n // tm,),
        in_specs=[
            pl.BlockSpec((tm, D_MODEL), row),
            pl.BlockSpec((tm, C_Q_WIDTH), row),
            pl.BlockSpec(wout.shape, const),
            pl.BlockSpec((1, D_MODEL), const),
            pl.BlockSpec(rw_hi.shape, const),
            pl.BlockSpec(rw_lo.shape, const),
        ],
        out_specs=(pl.BlockSpec((tm, D_MODEL), row), pl.BlockSpec((tm, D_MODEL // 2), row),
                   pl.BlockSpec((tm, LANES), row), pl.BlockSpec((8, tm), lambda i: (0, i))),
        compiler_params=_cparams(("arbitrary",)),
        name="odd_out_router",
    )(x, attn, wout, ln, rw_hi, rw_lo)


def _pack_pairs(lo, hi):
    lo_bits = lax.bitcast_convert_type(lo.astype(BF16).astype(F32), jnp.uint32) >> 16
    hi_bits = lax.bitcast_convert_type(hi.astype(BF16).astype(F32), jnp.uint32) & jnp.uint32(0xFFFF0000)
    return lo_bits | hi_bits


def _unpack_pairs(u):
    lo = lax.bitcast_convert_type(u << 16, F32)
    hi = lax.bitcast_convert_type(u & jnp.uint32(0xFFFF0000), F32)
    return lo, hi


def _rank_body(route_ref, tri_ref, pos_ref, meta_ref, cnt_ref, off_ref, *, tile_rows):
    phase = pl.program_id(0)
    i = pl.program_id(1)
    tm = route_ref.shape[1]
    route = route_ref[...]
    i1 = route[0:1, :]
    i2 = route[1:2, :]
    row = lax.broadcasted_iota(jnp.int32, (N_EXPERTS, tm), 0).astype(F32)
    hit = ((row == i1) | (row == i2)).astype(F32)
    per_expert = jnp.sum(hit, axis=1, keepdims=True)
    erow = lax.broadcasted_iota(jnp.int32, (N_EXPERTS, 1), 0)

    @pl.when((phase == 0) & (i == 0))
    def _():
        cnt_ref[...] = jnp.zeros_like(cnt_ref)

    @pl.when(phase == 0)
    def _():
        cnt_ref[...] += per_expert

    @pl.when((phase == 1) & (i == 0))
    def _():
        tot = cnt_ref[...]
        padded = jnp.floor((tot + (tile_rows - 1)) * (1.0 / tile_rows)) * tile_rows
        off = jnp.zeros_like(tot)
        for e in range(N_EXPERTS - 1):
            pe = jnp.sum(jnp.where(erow == e, padded, 0.0), axis=0, keepdims=True)
            off = off + jnp.where(erow > e, pe, 0.0)
        off_ref[...] = off
        lane = lax.broadcasted_iota(jnp.int32, meta_ref.shape, 1)
        meta_ref[...] = jnp.where(lane == 0, off, jnp.where(lane == 1, tot, 0.0))
        cnt_ref[...] = jnp.zeros_like(cnt_ref)

    @pl.when(phase == 1)
    def _():
        hit16 = jnp.concatenate([hit, jnp.zeros_like(hit)], axis=0).astype(BF16)
        before = jnp.dot(hit16, tri_ref[...], preferred_element_type=F32)[0:N_EXPERTS, :]
        slot_row = off_ref[...] + cnt_ref[...] + before
        pos1 = jnp.sum(jnp.where(row == i1, slot_row, 0.0), axis=0, keepdims=True)
        pos2 = jnp.sum(jnp.where(row == i2, slot_row, 0.0), axis=0, keepdims=True)
        pos_ref[...] = jnp.concatenate([pos1, pos2, jnp.zeros((6, tm), F32)], axis=0).astype(jnp.int32)
        cnt_ref[...] += per_expert


def _rank(route_t, tri, tm, tile_rows):
    n = route_t.shape[1]
    return pl.pallas_call(
        functools.partial(_rank_body, tile_rows=tile_rows),
        out_shape=(jax.ShapeDtypeStruct((n // tm, 8, tm), jnp.int32), jax.ShapeDtypeStruct((N_EXPERTS, LANES), F32)),
        grid=(2, n // tm),
        in_specs=[pl.BlockSpec((8, tm), lambda ph, i: (0, i)), pl.BlockSpec(tri.shape, lambda ph, i: (0, 0))],
        out_specs=(pl.BlockSpec((None, 8, tm), lambda ph, i: (i * ph, 0, 0)),
                   pl.BlockSpec((N_EXPERTS, LANES), lambda ph, i: (0, 0))),
        scratch_shapes=[pltpu.VMEM((N_EXPERTS, 1), F32), pltpu.VMEM((N_EXPERTS, 1), F32)],
        compiler_params=_cparams(("arbitrary", "arbitrary")),
        name="moe_rank",
    )(route_t, tri)


def _row_copy(src, src_row, dst, dst_row, sem):
    return pltpu.make_async_copy(src.at[pl.ds(src_row, 1)], dst.at[pl.ds(dst_row, 1)], sem)


def _dispatch_body(pos_ref, xn_ref, buf_ref, xs_ref, sem, *, tm):
    del buf_ref
    base = pl.program_id(0) * tm

    def issue(t, carry):
        _row_copy(xn_ref, base + t, xs_ref, pos_ref[0, t], sem).start()
        _row_copy(xn_ref, base + t, xs_ref, pos_ref[1, t], sem).start()
        return carry

    lax.fori_loop(0, tm, issue, 0, unroll=8)

    def drain(t, carry):
        _row_copy(xn_ref, 0, xs_ref, 0, sem).wait()
        _row_copy(xn_ref, 0, xs_ref, 0, sem).wait()
        return carry

    lax.fori_loop(0, tm, drain, 0, unroll=8)


def _dispatch(pos, xn, rows, tm):
    n = xn.shape[0]
    buf = jnp.zeros((rows, xn.shape[1]), xn.dtype)
    return pl.pallas_call(
        functools.partial(_dispatch_body, tm=tm),
        out_shape=jax.ShapeDtypeStruct(buf.shape, buf.dtype),
        grid=(n // tm,),
        in_specs=[pl.BlockSpec((None, 8, tm), lambda i: (i, 0, 0), memory_space=pltpu.SMEM),
                  pl.BlockSpec(memory_space=pl.ANY), pl.BlockSpec(memory_space=pl.ANY)],
        out_specs=pl.BlockSpec(memory_space=pl.ANY),
        scratch_shapes=[pltpu.SemaphoreType.DMA(())],
        input_output_aliases={2: 0},
        compiler_params=_cparams(("arbitrary",)),
        name="moe_dispatch",
    )(pos, xn, buf)


def _collect_body(pos_ref, ys_ref, y1_ref, y2_ref, sem, *, tm):
    base = pl.program_id(0) * tm

    def issue(t, carry):
        _row_copy(ys_ref, pos_ref[0, t], y1_ref, base + t, sem).start()
        _row_copy(ys_ref, pos_ref[1, t], y2_ref, base + t, sem).start()
        return carry

    lax.fori_loop(0, tm, issue, 0, unroll=8)

    def drain(t, carry):
        _row_copy(ys_ref, 0, y1_ref, 0, sem).wait()
        _row_copy(ys_ref, 0, y2_ref, 0, sem).wait()
        return carry

    lax.fori_loop(0, tm, drain, 0, unroll=8)


def _collect(pos, ys, n, tm):
    shape = jax.ShapeDtypeStruct((n, ys.shape[1]), ys.dtype)
    return pl.pallas_call(
        functools.partial(_collect_body, tm=tm),
        out_shape=(shape, shape),
        grid=(n // tm,),
        in_specs=[pl.BlockSpec((None, 8, tm), lambda i: (i, 0, 0), memory_space=pltpu.SMEM),
                  pl.BlockSpec(memory_space=pl.ANY)],
        out_specs=(pl.BlockSpec(memory_space=pl.ANY), pl.BlockSpec(memory_space=pl.ANY)),
        scratch_shapes=[pltpu.SemaphoreType.DMA(())],
        compiler_params=_cparams(("arbitrary",)),
        name="moe_collect",
    )(pos, ys)


def _expert_ffn_body(te_ref, tv_ref, xs_ref, wg_ref, wu_ref, wd_ref, ys_ref, xa_ref, xb_ref, acc_ref):
    del te_ref
    j = pl.program_id(0)
    c = pl.program_id(1)
    half = D_MODEL // 2

    @pl.when(c == 0)
    def _():
        lo, hi = _unpack_pairs(xs_ref[...])
        xa_ref[...] = lo.astype(BF16)
        xb_ref[...] = hi.astype(BF16)
        acc_ref[...] = jnp.zeros_like(acc_ref)

    @pl.when(tv_ref[j] != 0)
    def _():
        xa, xb = xa_ref[...], xb_ref[...]
        g = (jnp.dot(xa, wg_ref[0:half, :], preferred_element_type=F32)
             + jnp.dot(xb, wg_ref[half:, :], preferred_element_type=F32))
        up = (jnp.dot(xa, wu_ref[0:half, :], preferred_element_type=F32)
              + jnp.dot(xb, wu_ref[half:, :], preferred_element_type=F32))
        h = (g * jax.nn.sigmoid(g) * up).astype(BF16)
        acc_ref[...] += jnp.dot(h, wd_ref[...], preferred_element_type=F32)

    @pl.when(c == pl.num_programs(1) - 1)
    def _():
        acc = acc_ref[...]
        ys_ref[...] = _pack_pairs(acc[:, :half], acc[:, half:])


def _expert_ffn(tile_expert, tile_valid, xs, wg, wu, wd, tile_rows, chunk):
    rows, half = xs.shape
    n_chunks = D_EXPERT // chunk

    def chunk_of(j, c, tv):
        return jnp.where(tv[j] != 0, c, n_chunks - 1)

    grid_spec = pltpu.PrefetchScalarGridSpec(
        num_scalar_prefetch=2,
        grid=(rows // tile_rows, n_chunks),
        in_specs=[
            pl.BlockSpec((tile_rows, half), lambda j, c, te, tv: (j, 0)),
            pl.BlockSpec((None, D_MODEL, chunk), lambda j, c, te, tv: (te[j], 0, chunk_of(j, c, tv))),
            pl.BlockSpec((None, D_MODEL, chunk), lambda j, c, te, tv: (te[j], 0, chunk_of(j, c, tv))),
            pl.BlockSpec((None, chunk, D_MODEL), lambda j, c, te, tv: (te[j], chunk_of(j, c, tv), 0)),
        ],
        out_specs=pl.BlockSpec((tile_rows, half), lambda j, c, te, tv: (j, 0)),
        scratch_shapes=[pltpu.VMEM((tile_rows, half), BF16), pltpu.VMEM((tile_rows, half), BF16),
                        pltpu.VMEM((tile_rows, D_MODEL), F32)],
    )
    return pl.pallas_call(
        _expert_ffn_body,
        out_shape=jax.ShapeDtypeStruct(xs.shape, xs.dtype),
        grid_spec=grid_spec,
        compiler_params=_cparams(("arbitrary", "arbitrary")),
        name="moe_expert_ffn",
    )(tile_expert, tile_valid, xs, wg, wu, wd)


def _moe_routed(xn, route_t, wg, wu, wd):
    n = xn.shape[0]
    tm = MOE_TOKEN_TILE
    tile_rows = MOE_ROW_TILE
    rows = TOP_K * n + N_EXPERTS * tile_rows
    tri = jnp.asarray(np.triu(np.ones((tm, tm), np.float32), k=1), dtype=BF16)
    pos, meta = _rank(route_t, tri, tm, tile_rows)
    seg_end = meta[:, 0] + jnp.ceil(meta[:, 1] / tile_rows) * tile_rows
    tile_start = (jnp.arange(rows // tile_rows) * tile_rows).astype(F32)
    tile_expert = jnp.minimum(jnp.sum(tile_start[:, None] >= seg_end[None, :], axis=1), N_EXPERTS - 1).astype(jnp.int32)
    tile_valid = (tile_start < seg_end[N_EXPERTS - 1]).astype(jnp.int32)
    xs = _dispatch(pos, xn, rows, tm)
    ys = _expert_ffn(tile_expert, tile_valid, xs, wg, wu, wd, tile_rows, MOE_FF_CHUNK)
    return _collect(pos, ys, n, tm)


def _moe_ple_body(x_ref, y1_ref, y2_ref, route_ref, p_ref, g_ref, wp_ref, wg_ref, out_ref):
    route = route_ref[...]
    w1 = route[:, 2:3]
    w2 = route[:, 3:4]
    lo1, hi1 = _unpack_pairs(y1_ref[...])
    lo2, hi2 = _unpack_pairs(y2_ref[...])
    moe = jnp.concatenate([w1 * lo1 + w2 * lo2, w1 * hi1 + w2 * hi2], axis=1)
    x = x_ref[...] + moe
    xn = _rms(x, g_ref[...]).astype(BF16)
    gate = jax.nn.sigmoid(jnp.dot(xn, wg_ref[...], preferred_element_type=F32))
    emb = jnp.dot(p_ref[...].astype(BF16), wp_ref[...], preferred_element_type=F32)
    out_ref[...] = x + emb * gate


def _moe_ple(x, y1, y2, route, p, ln, wp, wg, tm):
    n = x.shape[0]
    row = lambda i: (i, 0)
    const = lambda i: (0, 0)
    return pl.pallas_call(
        _moe_ple_body,
        out_shape=jax.ShapeDtypeStruct((n, D_MODEL), F32),
        grid=(n // tm,),
        in_specs=[
            pl.BlockSpec((tm, D_MODEL), row),
            pl.BlockSpec((tm, D_MODEL // 2), row),
            pl.BlockSpec((tm, D_MODEL // 2), row),
            pl.BlockSpec((tm, LANES), row),
            pl.BlockSpec((tm, PLE_DIM), row),
            pl.BlockSpec((1, D_MODEL), const),
            pl.BlockSpec(wp.shape, const),
            pl.BlockSpec(wg.shape, const),
        ],
        out_specs=pl.BlockSpec((tm, D_MODEL), row),
        compiler_params=_cparams(("arbitrary",)),
        name="moe_combine_embed",
    )(x, y1, y2, route, p, ln, wp, wg)


def _rope_tables(pos, dim):
    inv = ROPE_THETA ** (-jnp.arange(0, dim, 2, dtype=F32) / dim)
    ang = pos.astype(F32)[:, None] * inv[None, :]
    ang = jnp.concatenate([ang, ang], axis=-1)
    return jnp.cos(ang), jnp.sin(ang)


def _tile_heads(t):
    return jnp.tile(t, (1, HEADS_PER_BLOCK))


def _even_rope(seq):
    cos, sin = _rope_tables(jnp.arange(seq), HEAD_DIM)
    sign = jnp.where(jnp.arange(HEAD_DIM) < HEAD_DIM // 2, -1.0, 1.0).astype(F32)
    return _tile_heads(cos), _tile_heads(sin * sign)


def _odd_rope(seq):
    t = jnp.arange(seq)
    half = HEAD_DIM // 2
    cr, sr = _rope_tables(t // GRID_W, half)
    cc, sc = _rope_tables(t % GRID_W, half)
    sign = jnp.where(jnp.arange(half) < half // 2, -1.0, 1.0).astype(F32)
    cos = jnp.concatenate([cr, cc], axis=-1)
    sin = jnp.concatenate([sr * sign, sc * sign], axis=-1)
    return _tile_heads(cos), _tile_heads(sin)


def _head_mean_matrix():
    blk = np.arange(HEAD_BLOCK) // HEAD_DIM
    return jnp.asarray((blk[:, None] == blk[None, :]).astype(np.float32) / HEAD_DIM, dtype=BF16)


def _q_head_permutation():
    n_blocks = C_Q_WIDTH // HEAD_BLOCK
    heads = [HEADS_PER_BLOCK * g + c for c in range(n_blocks) for g in range(HEADS_PER_BLOCK)]
    return np.concatenate([np.arange(h * HEAD_DIM, (h + 1) * HEAD_DIM) for h in heads])


def _row(v):
    return v.astype(F32).reshape(1, -1)


def _even_layer(x, batch, seq, w):
    tm = 512
    cos, sin = _even_rope(seq)
    qkv, u = _proj_even(x, w['ln_mix'], w['w_in'], w['bd'], w['qn'], w['kn'], cos, sin, seq, tm)
    outs, lses = [], []
    for g in range(len(A_GROUPS)):
        o, lse = _dilated_attention(qkv, batch, seq, g)
        outs.append(o)
        lses.append(lse)
    ssm_tm = 128
    ys = []
    for direction in range(2):
        tabs = _ssm_tables(*(w[k][direction] for k in ('lam_re', 'lam_im', 'log_dt', 'b_re', 'b_im', 'c_re', 'c_im')),
                           ssm_tm, direction == 1)
        ys.append(_ssm_scan(u, *tabs, batch, seq, ssm_tm, direction == 1))
    x = _even_out(x, outs, lses, u, ys[0], ys[1], w['d_skip'], w['w_glu'], w['w_out'], tm)
    return _ffn(x, w['ln_ffn'], w['ffn_gate'], w['ffn_up'], w['ffn_down'], 256)


def _odd_layer(x, p, batch, seq, w, ple):
    tm = 512
    cos, sin = _odd_rope(seq)
    qt, k, vt = _proj_odd(x, w['ln_mix'], w['w_in'], w['bd'], w['qn'], w['kn'], cos, sin, seq, tm)
    attn = _gqa_attention(qt, k, vt, batch, seq, 256, tm, tm)
    x, xn, route, route_t = _odd_out(x, attn, w['w_out'], w['ln_ffn'], w['router_hi'], w['router_lo'],
                                     MOE_TOKEN_TILE)
    y1, y2 = _moe_routed(xn, route_t, w['moe_gate'], w['moe_up'], w['moe_down'])
    return _moe_ple(x, y1, y2, route, p, *ple, tm)


def _prepare_even(j, ln_mix_e, w_in_e, a_qnorm, a_knorm, lam_re, lam_im, log_dt, b_re, b_im, c_re, c_im,
                  ssm_d, ssm_w_glu, w_out_e, ln_ffn_e, ffn_w_gate, ffn_w_up, ffn_w_down):
    return dict(
        ln_mix=_row(ln_mix_e[j]), w_in=w_in_e[j].astype(BF16), bd=_head_mean_matrix(),
        qn=_tile_heads(_row(a_qnorm[j])), kn=_tile_heads(_row(a_knorm[j])),
        lam_re=lam_re[j], lam_im=lam_im[j], log_dt=log_dt[j], b_re=b_re[j], b_im=b_im[j],
        c_re=c_re[j], c_im=c_im[j], d_skip=_row(ssm_d[j]), w_glu=ssm_w_glu[j].astype(BF16),
        w_out=w_out_e[j].astype(BF16), ln_ffn=_row(ln_ffn_e[j]), ffn_gate=ffn_w_gate[j].astype(BF16),
        ffn_up=ffn_w_up[j].astype(BF16), ffn_down=ffn_w_down[j].astype(BF16))


def _prepare_odd(j, ln_mix_o, w_in_o, c_qnorm, c_knorm, w_out_o, ln_ffn_o, router_w, moe_w_gate, moe_w_up,
                 moe_w_down):
    perm = _q_head_permutation()
    w_in = w_in_o[j]
    w_in = jnp.concatenate([w_in[:, :C_Q_WIDTH][:, perm], w_in[:, C_Q_WIDTH:]], axis=1).astype(BF16)
    rw = jnp.pad(router_w[j].astype(F32), ((0, 0), (0, LANES - N_EXPERTS)))
    rw_hi = rw.astype(BF16)
    rw_lo = (rw - rw_hi.astype(F32)).astype(BF16)
    return dict(
        ln_mix=_row(ln_mix_o[j]), w_in=w_in, bd=_head_mean_matrix(),
        qn=_tile_heads(_row(c_qnorm[j])), kn=_tile_heads(_row(c_knorm[j])),
        w_out=w_out_o[j][perm, :].astype(BF16), ln_ffn=_row(ln_ffn_o[j]), router_hi=rw_hi, router_lo=rw_lo,
        moe_gate=moe_w_gate[j].astype(BF16), moe_up=moe_w_up[j].astype(BF16), moe_down=moe_w_down[j].astype(BF16))


def _trunk(x, p, layers, ple):
    batch, seq, _ = x.shape
    x = x.reshape(batch * seq, D_MODEL)
    for i, w in enumerate(layers):
        pi = p[i].reshape(batch * seq, PLE_DIM)
        if i % 2 == 0:
            x = _ple(_even_layer(x, batch, seq, w), pi, *ple[i], 512)
        else:
            x = _odd_layer(x, pi, batch, seq, w, ple[i])
    return x.reshape(batch, seq, D_MODEL)


def kernel(x_prompt, x_sample, p_prompt, p_sample, ln_mix_e, w_in_e, a_qnorm, a_knorm, ssm_lam_re, ssm_lam_im, ssm_log_dt, ssm_b_re, ssm_b_im, ssm_c_re, ssm_c_im, ssm_d, ssm_w_glu, w_out_e, ln_ffn_e, ffn_w_gate, ffn_w_up, ffn_w_down, ln_mix_o, w_in_o, c_qnorm, c_knorm, w_out_o, ln_ffn_o, router_w, moe_w_gate, moe_w_up, moe_w_down, ple_ln, ple_w_proj, ple_w_gate):
    depth = p_prompt.shape[0]
    layers = []
    for i in range(depth):
        j = i // 2
        if i % 2 == 0:
            layers.append(_prepare_even(j, ln_mix_e, w_in_e, a_qnorm, a_knorm, ssm_lam_re, ssm_lam_im, ssm_log_dt,
                                        ssm_b_re, ssm_b_im, ssm_c_re, ssm_c_im, ssm_d, ssm_w_glu, w_out_e, ln_ffn_e,
                                        ffn_w_gate, ffn_w_up, ffn_w_down))
        else:
            layers.append(_prepare_odd(j, ln_mix_o, w_in_o, c_qnorm, c_knorm, w_out_o, ln_ffn_o, router_w,
                                       moe_w_gate, moe_w_up, moe_w_down))
    ple = [(_row(ple_ln[i]), ple_w_proj[i].astype(BF16), ple_w_gate[i].astype(BF16)) for i in range(depth)]
    y_prompt = _trunk(x_prompt, p_prompt, layers, ple)
    y_sample = _trunk(x_sample, p_sample, layers, ple)
    return (y_prompt, y_sample)
```

```python
import functools
import math

import jax
import jax.numpy as jnp
import numpy as np
from jax import lax
from jax.experimental import pallas as pl
from jax.experimental.pallas import tpu as pltpu

F32 = jnp.float32
BF16 = jnp.bfloat16

D_MODEL = 1024
HEAD_DIM = 64
EPS = 1e-6
ROPE_THETA = 10000.0
NEG_INF = -1e30
LOG2_E = math.log2(math.e)
LANES = 128
HEAD_BLOCK = 256
HEADS_PER_BLOCK = HEAD_BLOCK // HEAD_DIM
A_GROUPS = ((128, 1), (512, 4), (2048, 16))
A_HALF = 64
A_WIDTH = 768
B_WIDTH = 256
B_GROUP_CH = 16
B_GROUPS = 16
B_STATE = 64
SSM_N = B_GROUPS * B_STATE
QKV_E = 3 * A_WIDTH
D_FF = 2816
GRID_W = 64
C_Q_WIDTH = 1024
C_KV_WIDTH = 256
V_ONES_ROWS = 16
V_HEAD_ROWS = HEAD_DIM + V_ONES_ROWS
N_EXPERTS = 8
TOP_K = 2
D_EXPERT = 3584
MOE_TOKEN_TILE = 512
MOE_ROW_TILE = 512
MOE_FF_CHUNK = 512
PLE_DIM = 256

VMEM_LIMIT = 56 * 1024 * 1024


def _cparams(sem):
    return pltpu.CompilerParams(dimension_semantics=sem, vmem_limit_bytes=VMEM_LIMIT)


def _rms(x, g):
    ms = jnp.mean(x * x, axis=-1, keepdims=True)
    return x * lax.rsqrt(ms + EPS) * g


def _headnorm_rope(y, bd, gain, cos, sin_signed, half):
    ss = jnp.dot((y * y).astype(BF16), bd, preferred_element_type=F32)
    yn = y * lax.rsqrt(ss + EPS) * gain
    width = yn.shape[1]
    lane = lax.broadcasted_iota(jnp.int32, yn.shape, 1)
    up = pltpu.roll(yn, width - half, 1)
    dn = pltpu.roll(yn, half, 1)
    rot = jnp.where((lane & (2 * half - 1)) < half, up, dn)
    return yn * cos + rot * sin_signed


def _proj_even_body(x_ref, g_ref, w_ref, bd_ref, qn_ref, kn_ref, cos_ref, sin_ref, qkv_ref, u_ref):
    xn = _rms(x_ref[...], g_ref[...]).astype(BF16)
    n_head_blocks = A_WIDTH // HEAD_BLOCK
    for c in range(QKV_E // HEAD_BLOCK + 1):
        y = jnp.dot(xn, w_ref[:, c * HEAD_BLOCK:(c + 1) * HEAD_BLOCK], preferred_element_type=F32)
        if c < 2 * n_head_blocks:
            gain = qn_ref[...] if c < n_head_blocks else kn_ref[...]
            y = _headnorm_rope(y, bd_ref[...], gain, cos_ref[...], sin_ref[...], HEAD_DIM // 2)
            if c < n_head_blocks:
                y = y * (HEAD_DIM ** -0.5)
        if c < QKV_E // HEAD_BLOCK:
            qkv_ref[:, c * HEAD_BLOCK:(c + 1) * HEAD_BLOCK] = y.astype(BF16)
        else:
            u_ref[...] = y


def _proj_even(x, ln, w, bd, qn, kn, cos, sin, seq, tm):
    n = x.shape[0]
    n_seq_tiles = seq // tm
    const = lambda i: (0, 0)
    return pl.pallas_call(
        _proj_even_body,
        out_shape=(jax.ShapeDtypeStruct((n, QKV_E), BF16), jax.ShapeDtypeStruct((n, B_WIDTH), F32)),
        grid=(n // tm,),
        in_specs=[
            pl.BlockSpec((tm, D_MODEL), lambda i: (i, 0)),
            pl.BlockSpec((1, D_MODEL), const),
            pl.BlockSpec(w.shape, const),
            pl.BlockSpec(bd.shape, const),
            pl.BlockSpec((1, HEAD_BLOCK), const),
            pl.BlockSpec((1, HEAD_BLOCK), const),
            pl.BlockSpec((tm, HEAD_BLOCK), lambda i: (i % n_seq_tiles, 0)),
            pl.BlockSpec((tm, HEAD_BLOCK), lambda i: (i % n_seq_tiles, 0)),
        ],
        out_specs=(pl.BlockSpec((tm, QKV_E), lambda i: (i, 0)), pl.BlockSpec((tm, B_WIDTH), lambda i: (i, 0))),
        compiler_params=_cparams(("arbitrary",)),
        name="proj_even",
    )(x, ln, w, bd, qn, kn, cos, sin)


def _proj_odd_body(x_ref, g_ref, w_ref, bd_ref, qn_ref, kn_ref, cos_ref, sin_ref, qt_ref, k_ref, vt_ref):
    xn = _rms(x_ref[...], g_ref[...]).astype(BF16)
    tm = xn.shape[0]
    n_q_blocks = C_Q_WIDTH // HEAD_BLOCK
    for c in range(n_q_blocks + 2):
        y = jnp.dot(xn, w_ref[:, c * HEAD_BLOCK:(c + 1) * HEAD_BLOCK], preferred_element_type=F32)
        if c <= n_q_blocks:
            gain = qn_ref[...] if c < n_q_blocks else kn_ref[...]
            y = _headnorm_rope(y, bd_ref[...], gain, cos_ref[...], sin_ref[...], HEAD_DIM // 4)
        if c < n_q_blocks:
            qt_ref[c * HEAD_BLOCK:(c + 1) * HEAD_BLOCK, :] = (y * (HEAD_DIM ** -0.5 * LOG2_E)).T.astype(BF16)
        elif c == n_q_blocks:
            k_ref[...] = y.astype(BF16)
        else:
            yt = y.T.astype(BF16)
            ones = jnp.ones((V_ONES_ROWS, tm), BF16)
            for g in range(HEADS_PER_BLOCK):
                vt_ref[g * V_HEAD_ROWS:g * V_HEAD_ROWS + HEAD_DIM, :] = yt[g * HEAD_DIM:(g + 1) * HEAD_DIM, :]
                vt_ref[g * V_HEAD_ROWS + HEAD_DIM:(g + 1) * V_HEAD_ROWS, :] = ones


def _proj_odd(x, ln, w, bd, qn, kn, cos, sin, seq, tm):
    n = x.shape[0]
    n_seq_tiles = seq // tm
    const = lambda i: (0, 0)
    return pl.pallas_call(
        _proj_odd_body,
        out_shape=(jax.ShapeDtypeStruct((n // tm, C_Q_WIDTH, tm), BF16),
                   jax.ShapeDtypeStruct((n, C_KV_WIDTH), BF16),
                   jax.ShapeDtypeStruct((n // tm, HEADS_PER_BLOCK * V_HEAD_ROWS, tm), BF16)),
        grid=(n // tm,),
        in_specs=[
            pl.BlockSpec((tm, D_MODEL), lambda i: (i, 0)),
            pl.BlockSpec((1, D_MODEL), const),
            pl.BlockSpec(w.shape, const),
            pl.BlockSpec(bd.shape, const),
            pl.BlockSpec((1, HEAD_BLOCK), const),
            pl.BlockSpec((1, HEAD_BLOCK), const),
            pl.BlockSpec((tm, HEAD_BLOCK), lambda i: (i % n_seq_tiles, 0)),
            pl.BlockSpec((tm, HEAD_BLOCK), lambda i: (i % n_seq_tiles, 0)),
        ],
        out_specs=(pl.BlockSpec((None, C_Q_WIDTH, tm), lambda i: (i, 0, 0)),
                   pl.BlockSpec((tm, C_KV_WIDTH), lambda i: (i, 0)),
                   pl.BlockSpec((None, HEADS_PER_BLOCK * V_HEAD_ROWS, tm), lambda i: (i, 0, 0))),
        compiler_params=_cparams(("arbitrary",)),
        name="proj_odd",
    )(x, ln, w, bd, qn, kn, cos, sin)


def _dilated_body(q_ref, k_ref, v_ref, o_ref, lse_ref, *, tq, kw, sub_len):
    i = pl.program_id(2)
    q0 = i * tq
    k0 = pl.multiple_of(jnp.clip(q0 - A_HALF, 0, sub_len - kw), A_HALF)
    q = q_ref[...]
    kwin = k_ref[pl.ds(k0, kw), :]
    vwin = v_ref[pl.ds(k0, kw), :]
    qpos = q0 + lax.broadcasted_iota(jnp.int32, (tq, kw), 0)
    kpos = k0 + lax.broadcasted_iota(jnp.int32, (tq, kw), 1)
    valid = jnp.abs(kpos - qpos) <= A_HALF
    lane_head = lax.broadcasted_iota(jnp.int32, (tq, HEAD_BLOCK), 1) // HEAD_DIM
    o = jnp.zeros((tq, HEAD_BLOCK), F32)
    lse = jnp.zeros((tq, HEAD_BLOCK), F32)
    for h in range(HEADS_PER_BLOCK):
        sel = lane_head == h
        qh = jnp.where(sel, q, jnp.zeros_like(q))
        s = lax.dot_general(qh, kwin, (((1,), (1,)), ((), ())), preferred_element_type=F32)
        s = jnp.where(valid, s, NEG_INF)
        m = jnp.max(s, axis=-1, keepdims=True)
        e = jnp.exp(s - m)
        den = jnp.sum(e, axis=-1, keepdims=True)
        p = (e / den).astype(BF16)
        oh = jnp.dot(p, vwin, preferred_element_type=F32)
        o = jnp.where(sel, oh, o)
        lse = jnp.where(sel, m + jnp.log(den), lse)
    o_ref[...] = o.astype(BF16)
    lse_ref[...] = lse


def _dilated_attention(qkv, batch, seq, group):
    _, dil = A_GROUPS[group]
    sub_len = seq // dil
    tq = min(128, sub_len)
    kw = min(256, sub_len)
    n_blk = QKV_E // HEAD_BLOCK
    view = qkv.reshape(batch, sub_len, dil * QKV_E)
    heads = A_WIDTH // HEAD_BLOCK
    o, lse = pl.pallas_call(
        functools.partial(_dilated_body, tq=tq, kw=kw, sub_len=sub_len),
        out_shape=(jax.ShapeDtypeStruct((batch, sub_len, dil * HEAD_BLOCK), BF16),
                   jax.ShapeDtypeStruct((batch, sub_len, dil * HEAD_BLOCK), F32)),
        grid=(batch, dil, sub_len // tq),
        in_specs=[
            pl.BlockSpec((None, tq, HEAD_BLOCK), lambda b, r, i: (b, i, r * n_blk + group)),
            pl.BlockSpec((None, sub_len, HEAD_BLOCK), lambda b, r, i: (b, 0, r * n_blk + heads + group)),
            pl.BlockSpec((None, sub_len, HEAD_BLOCK), lambda b, r, i: (b, 0, r * n_blk + 2 * heads + group)),
        ],
        out_specs=(pl.BlockSpec((None, tq, HEAD_BLOCK), lambda b, r, i: (b, i, r)),
                   pl.BlockSpec((None, tq, HEAD_BLOCK), lambda b, r, i: (b, i, r))),
        compiler_params=_cparams(("arbitrary", "arbitrary", "arbitrary")),
        name=f"dilated_attn_g{group}",
    )(view, view, view)
    return o.reshape(batch * seq, HEAD_BLOCK), lse.reshape(batch * seq, HEAD_BLOCK)


def _ssm_body(u_ref, bmat_ref, apow_ref, ptab_ref, cmat_ref, y_ref, carry_ref, *, tm, reverse):
    @pl.when(pl.program_id(1) == 0)
    def _():
        carry_ref[...] = jnp.zeros_like(carry_ref)

    u = u_ref[...].astype(BF16)
    bu = jnp.dot(u, bmat_ref[...], preferred_element_type=F32)
    row = lax.broadcasted_iota(jnp.int32, (tm, LANES), 0)
    n_steps = int(math.log2(tm))
    y = jnp.zeros((tm, B_WIDTH), F32)
    for j in range(SSM_N // LANES):
        lanes = slice(j * LANES, (j + 1) * LANES)
        hr = bu[:, j * LANES:(j + 1) * LANES]
        hi = bu[:, SSM_N + j * LANES:SSM_N + (j + 1) * LANES]
        for s in range(n_steps):
            k = 1 << s
            ar = apow_ref[s:s + 1, lanes]
            ai = apow_ref[n_steps + s:n_steps + s + 1, lanes]
            if reverse:
                keep = row < tm - k
                sr = jnp.where(keep, pltpu.roll(hr, tm - k, 0), 0.0)
                si = jnp.where(keep, pltpu.roll(hi, tm - k, 0), 0.0)
            else:
                keep = row >= k
                sr = jnp.where(keep, pltpu.roll(hr, k, 0), 0.0)
                si = jnp.where(keep, pltpu.roll(hi, k, 0), 0.0)
            hr, hi = hr + ar * sr - ai * si, hi + ar * si + ai * sr
        cr = carry_ref[0:1, lanes]
        ci = carry_ref[1:2, lanes]
        pr = ptab_ref[:, lanes]
        pi = ptab_ref[:, SSM_N + j * LANES:SSM_N + (j + 1) * LANES]
        hr, hi = hr + pr * cr - pi * ci, hi + pr * ci + pi * cr
        edge = 0 if reverse else tm - 1
        carry_ref[0:1, lanes] = hr[edge:edge + 1, :]
        carry_ref[1:2, lanes] = hi[edge:edge + 1, :]
        y = y + jnp.dot(hr.astype(BF16), cmat_ref[lanes, :], preferred_element_type=F32)
        y = y + jnp.dot(hi.astype(BF16), cmat_ref[SSM_N + j * LANES:SSM_N + (j + 1) * LANES, :],
                        preferred_element_type=F32)
    y_ref[...] = y


def _ssm_scan(u, bmat, apow, ptab, cmat, batch, seq, tm, reverse):
    n_t = seq // tm
    tmap = (lambda b, i: (b, n_t - 1 - i, 0)) if reverse else (lambda b, i: (b, i, 0))
    const = lambda b, i: (0, 0)
    return pl.pallas_call(
        functools.partial(_ssm_body, tm=tm, reverse=reverse),
        out_shape=jax.ShapeDtypeStruct((batch, seq, B_WIDTH), F32),
        grid=(batch, n_t),
        in_specs=[
            pl.BlockSpec((None, tm, B_WIDTH), tmap),
            pl.BlockSpec(bmat.shape, const),
            pl.BlockSpec(apow.shape, const),
            pl.BlockSpec(ptab.shape, const),
            pl.BlockSpec(cmat.shape, const),
        ],
        out_specs=pl.BlockSpec((None, tm, B_WIDTH), tmap),
        scratch_shapes=[pltpu.VMEM((8, SSM_N), F32)],
        compiler_params=_cparams(("arbitrary", "arbitrary")),
        name="ssm_scan_rev" if reverse else "ssm_scan_fwd",
    )(u.reshape(batch, seq, B_WIDTH), bmat, apow, ptab, cmat).reshape(batch * seq, B_WIDTH)


def _ssm_tables(lam_re, lam_im, log_dt, b_re, b_im, c_re, c_im, tm, reverse):
    lr, li = lam_re.astype(F32), lam_im.astype(F32)
    dt = jnp.exp(log_dt.astype(F32))[:, None]
    mag = jnp.exp(lr * dt)
    a_r = mag * jnp.cos(li * dt)
    a_i = mag * jnp.sin(li * dt)
    den = lr * lr + li * li
    f_r = ((a_r - 1.0) * lr + a_i * li) / den
    f_i = (a_i * lr - (a_r - 1.0) * li) / den
    br, bi = b_re.astype(F32), b_im.astype(F32)
    bb_r = f_r[..., None] * br - f_i[..., None] * bi
    bb_i = f_r[..., None] * bi + f_i[..., None] * br
    eye = jnp.eye(B_GROUPS, dtype=F32)

    def in_mat(bb):
        return jnp.einsum('gpc,gh->gchp', bb, eye).reshape(B_WIDTH, SSM_N)

    def out_mat(c):
        return jnp.einsum('gcp,gh->gphc', c.astype(F32), eye).reshape(SSM_N, B_WIDTH)

    bmat = jnp.concatenate([in_mat(bb_r), in_mat(bb_i)], axis=1).astype(BF16)
    cmat = jnp.concatenate([out_mat(c_re), -out_mat(c_im)], axis=0).astype(BF16)

    def cpow(k):
        kk = k.astype(F32)[:, None]
        m = jnp.exp(kk * (lr * dt).reshape(1, SSM_N))
        ang = kk * (li * dt).reshape(1, SSM_N)
        return m * jnp.cos(ang), m * jnp.sin(ang)

    n_steps = int(math.log2(tm))
    sr, si = cpow(2 ** jnp.arange(n_steps))
    apow = jnp.concatenate([sr, si], axis=0)
    t = jnp.arange(tm)
    pr, pi = cpow(tm - t if reverse else t + 1)
    ptab = jnp.concatenate([pr, pi], axis=1)
    return bmat, apow, ptab, cmat


def _gelu_tanh(x):
    return 0.5 * x * (1.0 + jnp.tanh(math.sqrt(2.0 / math.pi) * (x + 0.044715 * (x * x * x))))


def _even_out_body(x_ref, o0_ref, o1_ref, o2_ref, l0_ref, l1_ref, l2_ref, u_ref, yf_ref, yb_ref,
                   d_ref, wglu_ref, wout_ref, out_ref):
    l0, l1, l2 = l0_ref[...], l1_ref[...], l2_ref[...]
    lmax = jnp.maximum(jnp.maximum(l0, l1), l2)
    e0, e1, e2 = jnp.exp(l0 - lmax), jnp.exp(l1 - lmax), jnp.exp(l2 - lmax)
    den = e0 + e1 + e2
    a = ((e0 / den) * o0_ref[...].astype(F32) + (e1 / den) * o1_ref[...].astype(F32)
         + (e2 / den) * o2_ref[...].astype(F32))
    y = u_ref[...] * d_ref[...] + yf_ref[...] + yb_ref[...]
    z = _gelu_tanh(y)
    z = z * jax.nn.sigmoid(jnp.dot(z.astype(BF16), wglu_ref[...], preferred_element_type=F32))
    acc = jnp.dot(a.astype(BF16), wout_ref[0:HEAD_BLOCK, :], preferred_element_type=F32)
    acc = acc + jnp.dot(z.astype(BF16), wout_ref[HEAD_BLOCK:, :], preferred_element_type=F32)
    out_ref[...] = x_ref[...] + acc


def _even_out(x, os_, ls_, u, yf, yb, d_skip, wglu, wout, tm):
    n = x.shape[0]
    row = lambda i: (i, 0)
    const = lambda i: (0, 0)
    narrow = pl.BlockSpec((tm, HEAD_BLOCK), row)
    return pl.pallas_call(
        _even_out_body,
        out_shape=jax.ShapeDtypeStruct((n, D_MODEL), F32),
        grid=(n // tm,),
        in_specs=[pl.BlockSpec((tm, D_MODEL), row)] + [narrow] * 9 + [
            pl.BlockSpec((1, B_WIDTH), const),
            pl.BlockSpec(wglu.shape, const),
            pl.BlockSpec(wout.shape, const),
        ],
        out_specs=pl.BlockSpec((tm, D_MODEL), row),
        compiler_params=_cparams(("arbitrary",)),
        name="even_out",
    )(x, *os_, *ls_, u, yf, yb, d_skip, wglu, wout)


def _ffn_body(x_ref, g_ref, wg_ref, wu_ref, wd_ref, out_ref, *, chunk):
    x = x_ref[...]
    xn = _rms(x, g_ref[...]).astype(BF16)
    acc = jnp.zeros(x.shape, F32)
    for c in range(D_FF // chunk):
        cols = slice(c * chunk, (c + 1) * chunk)
        g = jnp.dot(xn, wg_ref[:, cols], preferred_element_type=F32)
        up = jnp.dot(xn, wu_ref[:, cols], preferred_element_type=F32)
        h = (g * jax.nn.sigmoid(g) * up).astype(BF16)
        acc = acc + jnp.dot(h, wd_ref[cols, :], preferred_element_type=F32)
    out_ref[...] = x + acc


def _ffn(x, ln, wg, wu, wd, tm):
    n = x.shape[0]
    row = lambda i: (i, 0)
    const = lambda i: (0, 0)
    return pl.pallas_call(
        functools.partial(_ffn_body, chunk=256),
        out_shape=jax.ShapeDtypeStruct((n, D_MODEL), F32),
        grid=(n // tm,),
        in_specs=[
            pl.BlockSpec((tm, D_MODEL), row),
            pl.BlockSpec((1, D_MODEL), const),
            pl.BlockSpec(wg.shape, const),
            pl.BlockSpec(wu.shape, const),
            pl.BlockSpec(wd.shape, const),
        ],
        out_specs=pl.BlockSpec((tm, D_MODEL), row),
        compiler_params=_cparams(("arbitrary",)),
        name="ffn_dense",
    )(x, ln, wg, wu, wd)


def _ple_body(x_ref, p_ref, g_ref, wp_ref, wg_ref, out_ref):
    x = x_ref[...]
    xn = _rms(x, g_ref[...]).astype(BF16)
    gate = jax.nn.sigmoid(jnp.dot(xn, wg_ref[...], preferred_element_type=F32))
    emb = jnp.dot(p_ref[...].astype(BF16), wp_ref[...], preferred_element_type=F32)
    out_ref[...] = x + emb * gate


def _ple(x, p, ln, wp, wg, tm):
    n = x.shape[0]
    row = lambda i: (i, 0)
    const = lambda i: (0, 0)
    return pl.pallas_call(
        _ple_body,
        out_shape=jax.ShapeDtypeStruct((n, D_MODEL), F32),
        grid=(n // tm,),
        in_specs=[
            pl.BlockSpec((tm, D_MODEL), row),
            pl.BlockSpec((tm, PLE_DIM), row),
            pl.BlockSpec((1, D_MODEL), const),
            pl.BlockSpec(wp.shape, const),
            pl.BlockSpec(wg.shape, const),
        ],
        out_specs=pl.BlockSpec((tm, D_MODEL), row),
        compiler_params=_cparams(("arbitrary",)),
        name="per_layer_embed",
    )(x, p, ln, wp, wg)


def _gqa_body(qt_ref, k_ref, vt_ref, o_ref, qs_ref, s_ref, m_ref, acc_ref, *, tq, tk, seq):
    n_kv = seq // tk
    for c in range(C_Q_WIDTH // HEAD_BLOCK):
        qs_ref[...] = jnp.zeros(qs_ref.shape, BF16)
        for g in range(HEADS_PER_BLOCK):
            rows = slice(c * HEAD_BLOCK + g * HEAD_DIM, c * HEAD_BLOCK + (g + 1) * HEAD_DIM)
            qs_ref[g * HEAD_DIM:(g + 1) * HEAD_DIM, g * tq:(g + 1) * tq] = qt_ref[rows, :]
        m_ref[...] = jnp.full(m_ref.shape, NEG_INF, F32)
        acc_ref[...] = jnp.zeros(acc_ref.shape, F32)

        def scores(t, slot):
            k0 = pl.multiple_of(t * tk, tk)
            s_ref[slot] = jnp.dot(k_ref[pl.ds(k0, tk), :], qs_ref[...], preferred_element_type=F32)

        def consume(t, slot):
            s = s_ref[slot]
            m_old = m_ref[...]
            m_new = jnp.maximum(m_old, jnp.max(s, axis=0, keepdims=True))
            alpha = jnp.exp2(m_old - m_new)
            p = jnp.exp2(s - m_new).astype(BF16)
            m_ref[...] = m_new
            for g in range(HEADS_PER_BLOCK):
                cols = slice(g * tq, (g + 1) * tq)
                vt = vt_ref[t, g * V_HEAD_ROWS:(g + 1) * V_HEAD_ROWS, :]
                acc_ref[g] = alpha[:, cols] * acc_ref[g] + jnp.dot(vt, p[:, cols], preferred_element_type=F32)

        scores(0, 0)

        def step(u, carry):
            scores(2 * u + 1, 1)
            consume(2 * u, 0)
            scores(jnp.minimum(2 * u + 2, n_kv - 1), 0)
            consume(2 * u + 1, 1)
            return carry

        lax.fori_loop(0, n_kv // 2, step, 0)
        heads = [acc_ref[g, 0:HEAD_DIM, :] / acc_ref[g, HEAD_DIM:HEAD_DIM + 1, :] for g in range(HEADS_PER_BLOCK)]
        o_ref[:, c * HEAD_BLOCK:(c + 1) * HEAD_BLOCK] = jnp.concatenate(heads, axis=0).T.astype(BF16)


def _gqa_attention(qt, k, vt, batch, seq, tq, tk, tm):
    assert tm % tq == 0 and seq % tk == 0
    per_tm = tm // tq
    tiles_per_seq = seq // tm
    return pl.pallas_call(
        functools.partial(_gqa_body, tq=tq, tk=tk, seq=seq),
        out_shape=jax.ShapeDtypeStruct((batch, seq, C_Q_WIDTH), BF16),
        grid=(batch, seq // tq),
        in_specs=[
            pl.BlockSpec((None, C_Q_WIDTH, tq), lambda b, i: (b * tiles_per_seq + i // per_tm, 0, i % per_tm)),
            pl.BlockSpec((None, seq, C_KV_WIDTH), lambda b, i: (b, 0, 0)),
            pl.BlockSpec((seq // tk, HEADS_PER_BLOCK * V_HEAD_ROWS, tk), lambda b, i: (b, 0, 0)),
        ],
        out_specs=pl.BlockSpec((None, tq, C_Q_WIDTH), lambda b, i: (b, i, 0)),
        scratch_shapes=[pltpu.VMEM((HEAD_BLOCK, HEADS_PER_BLOCK * tq), BF16),
                        pltpu.VMEM((2, tk, HEADS_PER_BLOCK * tq), F32),
                        pltpu.VMEM((1, HEADS_PER_BLOCK * tq), F32),
                        pltpu.VMEM((HEADS_PER_BLOCK, V_HEAD_ROWS, tq), F32)],
        compiler_params=_cparams(("arbitrary", "arbitrary")),
        name="gqa_attention",
    )(qt, k.reshape(batch, seq, C_KV_WIDTH), vt).reshape(batch * seq, C_Q_WIDTH)


def _odd_out_body(x_ref, a_ref, wout_ref, g_ref, rw_hi_ref, rw_lo_ref, out_ref, xn_ref, route_ref, route_t_ref):
    x = x_ref[...] + jnp.dot(a_ref[...], wout_ref[...], preferred_element_type=F32)
    out_ref[...] = x
    h = _rms(x, g_ref[...])
    h_hi = h.astype(BF16)
    h_lo = (h - h_hi.astype(F32)).astype(BF16)
    logits = (jnp.dot(h_hi, rw_hi_ref[...], preferred_element_type=F32)
              + jnp.dot(h_hi, rw_lo_ref[...], preferred_element_type=F32)
              + jnp.dot(h_lo, rw_hi_ref[...], preferred_element_type=F32))
    lane = lax.broadcasted_iota(jnp.int32, logits.shape, 1)
    logits = jnp.where(lane < N_EXPERTS, logits, -jnp.inf)
    m1 = jnp.max(logits, axis=-1, keepdims=True)
    i1 = jnp.min(jnp.where(logits == m1, lane, LANES), axis=-1, keepdims=True)
    rest = jnp.where(lane == i1, -jnp.inf, logits)
    m2 = jnp.max(rest, axis=-1, keepdims=True)
    i2 = jnp.min(jnp.where(rest == m2, lane, LANES), axis=-1, keepdims=True)
    e2 = jnp.exp(m2 - m1)
    w1 = 1.0 / (1.0 + e2)
    w2 = e2 / (1.0 + e2)
    rec = jnp.where(lane == 0, i1.astype(F32),
                    jnp.where(lane == 1, i2.astype(F32), jnp.where(lane == 2, w1, jnp.where(lane == 3, w2, 0.0))))
    route_ref[...] = rec
    route_t_ref[...] = rec.T[0:8, :]
    xn_ref[...] = _pack_pairs(h_hi[:, :D_MODEL // 2].astype(F32), h_hi[:, D_MODEL // 2:].astype(F32))


def _odd_out(x, attn, wout, ln, rw_hi, rw_lo, tm):
    n = x.shape[0]
    row = lambda i: (i, 0)
    const = lambda i: (0, 0)
    return pl.pallas_call(
        _odd_out_body,
        out_shape=(jax.ShapeDtypeStruct((n, D_MODEL), F32), jax.ShapeDtypeStruct((n, D_MODEL // 2), jnp.uint32),
                   jax.ShapeDtypeStruct((n, LANES), F32), jax.ShapeDtypeStruct((8, n), F32)),
        grid=(n // tm,),
        in_specs=[
            pl.BlockSpec((tm, D_MODEL), row),
            pl.BlockSpec((tm, C_Q_WIDTH), row),
            pl.BlockSpec(wout.shape, const),
            pl.BlockSpec((1, D_MODEL), const),
            pl.BlockSpec(rw_hi.shape, const),
            pl.BlockSpec(rw_lo.shape, const),
        ],
        out_specs=(pl.BlockSpec((tm, D_MODEL), row), pl.BlockSpec((tm, D_MODEL // 2), row),
                   pl.BlockSpec((tm, LANES), row), pl.BlockSpec((8, tm), lambda i: (0, i))),
        compiler_params=_cparams(("arbitrary",)),
        name="odd_out_router",
    )(x, attn, wout, ln, rw_hi, rw_lo)


def _pack_pairs(lo, hi):
    lo_bits = lax.bitcast_convert_type(lo.astype(BF16).astype(F32), jnp.uint32) >> 16
    hi_bits = lax.bitcast_convert_type(hi.astype(BF16).astype(F32), jnp.uint32) & jnp.uint32(0xFFFF0000)
    return lo_bits | hi_bits


def _unpack_pairs(u):
    lo = lax.bitcast_convert_type(u << 16, F32)
    hi = lax.bitcast_convert_type(u & jnp.uint32(0xFFFF0000), F32)
    return lo, hi


def _rank_body(route_ref, tri_ref, pos_ref, meta_ref, cnt_ref, off_ref, *, tile_rows):
    phase = pl.program_id(0)
    i = pl.program_id(1)
    tm = route_ref.shape[1]
    route = route_ref[...]
    i1 = route[0:1, :]
    i2 = route[1:2, :]
    row = lax.broadcasted_iota(jnp.int32, (N_EXPERTS, tm), 0).astype(F32)
    hit = ((row == i1) | (row == i2)).astype(F32)
    per_expert = jnp.sum(hit, axis=1, keepdims=True)
    erow = lax.broadcasted_iota(jnp.int32, (N_EXPERTS, 1), 0)

    @pl.when((phase == 0) & (i == 0))
    def _():
        cnt_ref[...] = jnp.zeros_like(cnt_ref)

    @pl.when(phase == 0)
    def _():
        cnt_ref[...] += per_expert

    @pl.when((phase == 1) & (i == 0))
    def _():
        tot = cnt_ref[...]
        padded = jnp.floor((tot + (tile_rows - 1)) * (1.0 / tile_rows)) * tile_rows
        off = jnp.zeros_like(tot)
        for e in range(N_EXPERTS - 1):
            pe = jnp.sum(jnp.where(erow == e, padded, 0.0), axis=0, keepdims=True)
            off = off + jnp.where(erow > e, pe, 0.0)
        off_ref[...] = off
        lane = lax.broadcasted_iota(jnp.int32, meta_ref.shape, 1)
        meta_ref[...] = jnp.where(lane == 0, off, jnp.where(lane == 1, tot, 0.0))
        cnt_ref[...] = jnp.zeros_like(cnt_ref)

    @pl.when(phase == 1)
    def _():
        hit16 = jnp.concatenate([hit, jnp.zeros_like(hit)], axis=0).astype(BF16)
        before = jnp.dot(hit16, tri_ref[...], preferred_element_type=F32)[0:N_EXPERTS, :]
        slot_row = off_ref[...] + cnt_ref[...] + before
        pos1 = jnp.sum(jnp.where(row == i1, slot_row, 0.0), axis=0, keepdims=True)
        pos2 = jnp.sum(jnp.where(row == i2, slot_row, 0.0), axis=0, keepdims=True)
        pos_ref[...] = jnp.concatenate([pos1, pos2, jnp.zeros((6, tm), F32)], axis=0).astype(jnp.int32)
        cnt_ref[...] += per_expert


def _rank(route_t, tri, tm, tile_rows):
    n = route_t.shape[1]
    return pl.pallas_call(
        functools.partial(_rank_body, tile_rows=tile_rows),
        out_shape=(jax.ShapeDtypeStruct((n // tm, 8, tm), jnp.int32), jax.ShapeDtypeStruct((N_EXPERTS, LANES), F32)),
        grid=(2, n // tm),
        in_specs=[pl.BlockSpec((8, tm), lambda ph, i: (0, i)), pl.BlockSpec(tri.shape, lambda ph, i: (0, 0))],
        out_specs=(pl.BlockSpec((None, 8, tm), lambda ph, i: (i * ph, 0, 0)),
                   pl.BlockSpec((N_EXPERTS, LANES), lambda ph, i: (0, 0))),
        scratch_shapes=[pltpu.VMEM((N_EXPERTS, 1), F32), pltpu.VMEM((N_EXPERTS, 1), F32)],
        compiler_params=_cparams(("arbitrary", "arbitrary")),
        name="moe_rank",
    )(route_t, tri)


def _row_copy(src, src_row, dst, dst_row, sem):
    return pltpu.make_async_copy(src.at[pl.ds(src_row, 1)], dst.at[pl.ds(dst_row, 1)], sem)


def _dispatch_body(pos_ref, xn_ref, buf_ref, xs_ref, sem, *, tm):
    del buf_ref

    def issue(t, carry):
        _row_copy(xn_ref, t, xs_ref, pos_ref[0, t], sem).start()
        _row_copy(xn_ref, t, xs_ref, pos_ref[1, t], sem).start()
        return carry

    lax.fori_loop(0, tm, issue, 0, unroll=8)
    for _ in range(TOP_K):
        pltpu.make_async_copy(xn_ref, xs_ref.at[pl.ds(0, tm)], sem).wait()


def _dispatch(pos, xn, rows, tm):
    n = xn.shape[0]
    buf = jnp.zeros((rows, xn.shape[1]), xn.dtype)
    return pl.pallas_call(
        functools.partial(_dispatch_body, tm=tm),
        out_shape=jax.ShapeDtypeStruct(buf.shape, buf.dtype),
        grid=(n // tm,),
        in_specs=[pl.BlockSpec((None, 8, tm), lambda i: (i, 0, 0), memory_space=pltpu.SMEM),
                  pl.BlockSpec((tm, xn.shape[1]), lambda i: (i, 0)), pl.BlockSpec(memory_space=pl.ANY)],
        out_specs=pl.BlockSpec(memory_space=pl.ANY),
        scratch_shapes=[pltpu.SemaphoreType.DMA(())],
        input_output_aliases={2: 0},
        compiler_params=_cparams(("arbitrary",)),
        name="moe_dispatch",
    )(pos, xn, buf)


def _expert_ffn_body(te_ref, tv_ref, xs_ref, wg_ref, wu_ref, wd_ref, ys_ref, xa_ref, xb_ref, acc_ref):
    del te_ref
    j = pl.program_id(0)
    c = pl.program_id(1)
    half = D_MODEL // 2

    @pl.when(c == 0)
    def _():
        lo, hi = _unpack_pairs(xs_ref[...])
        xa_ref[...] = lo.astype(BF16)
        xb_ref[...] = hi.astype(BF16)
        acc_ref[...] = jnp.zeros_like(acc_ref)

    @pl.when(tv_ref[j] != 0)
    def _():
        xa, xb = xa_ref[...], xb_ref[...]
        g = (jnp.dot(xa, wg_ref[0:half, :], preferred_element_type=F32)
             + jnp.dot(xb, wg_ref[half:, :], preferred_element_type=F32))
        up = (jnp.dot(xa, wu_ref[0:half, :], preferred_element_type=F32)
              + jnp.dot(xb, wu_ref[half:, :], preferred_element_type=F32))
        h = (g * jax.nn.sigmoid(g) * up).astype(BF16)
        acc_ref[...] += jnp.dot(h, wd_ref[...], preferred_element_type=F32)

    @pl.when(c == pl.num_programs(1) - 1)
    def _():
        acc = acc_ref[...]
        ys_ref[...] = _pack_pairs(acc[:, :half], acc[:, half:])


def _expert_ffn(tile_expert, tile_valid, xs, wg, wu, wd, tile_rows, chunk):
    rows, half = xs.shape
    n_chunks = D_EXPERT // chunk

    def chunk_of(j, c, tv):
        return jnp.where(tv[j] != 0, c, n_chunks - 1)

    grid_spec = pltpu.PrefetchScalarGridSpec(
        num_scalar_prefetch=2,
        grid=(rows // tile_rows, n_chunks),
        in_specs=[
            pl.BlockSpec((tile_rows, half), lambda j, c, te, tv: (j, 0)),
            pl.BlockSpec((None, D_MODEL, chunk), lambda j, c, te, tv: (te[j], 0, chunk_of(j, c, tv))),
            pl.BlockSpec((None, D_MODEL, chunk), lambda j, c, te, tv: (te[j], 0, chunk_of(j, c, tv))),
            pl.BlockSpec((None, chunk, D_MODEL), lambda j, c, te, tv: (te[j], chunk_of(j, c, tv), 0)),
        ],
        out_specs=pl.BlockSpec((tile_rows, half), lambda j, c, te, tv: (j, 0)),
        scratch_shapes=[pltpu.VMEM((tile_rows, half), BF16), pltpu.VMEM((tile_rows, half), BF16),
                        pltpu.VMEM((tile_rows, D_MODEL), F32)],
    )
    return pl.pallas_call(
        _expert_ffn_body,
        out_shape=jax.ShapeDtypeStruct(xs.shape, xs.dtype),
        grid_spec=grid_spec,
        compiler_params=_cparams(("arbitrary", "arbitrary")),
        name="moe_expert_ffn",
    )(tile_expert, tile_valid, xs, wg, wu, wd)


def _moe_routed(xn, route_t, wg, wu, wd):
    n = xn.shape[0]
    tm = MOE_TOKEN_TILE
    tile_rows = MOE_ROW_TILE
    rows = TOP_K * n + N_EXPERTS * tile_rows
    tri = jnp.asarray(np.triu(np.ones((tm, tm), np.float32), k=1), dtype=BF16)
    pos, meta = _rank(route_t, tri, tm, tile_rows)
    seg_end = meta[:, 0] + jnp.ceil(meta[:, 1] / tile_rows) * tile_rows
    tile_start = (jnp.arange(rows // tile_rows) * tile_rows).astype(F32)
    tile_expert = jnp.minimum(jnp.sum(tile_start[:, None] >= seg_end[None, :], axis=1), N_EXPERTS - 1).astype(jnp.int32)
    tile_valid = (tile_start < seg_end[N_EXPERTS - 1]).astype(jnp.int32)
    xs = _dispatch(pos, xn, rows, tm)
    return pos, _expert_ffn(tile_expert, tile_valid, xs, wg, wu, wd, tile_rows, MOE_FF_CHUNK)


def _moe_ple_body(pos_ref, ys_ref, x_ref, route_ref, p_ref, g_ref, wp_ref, wg_ref, out_ref, y1_ref, y2_ref, sem):
    tm = x_ref.shape[0]

    def issue(t, carry):
        _row_copy(ys_ref, pos_ref[0, t], y1_ref, t, sem).start()
        _row_copy(ys_ref, pos_ref[1, t], y2_ref, t, sem).start()
        return carry

    lax.fori_loop(0, tm, issue, 0, unroll=8)
    pltpu.make_async_copy(ys_ref.at[pl.ds(0, tm)], y1_ref, sem).wait()
    pltpu.make_async_copy(ys_ref.at[pl.ds(0, tm)], y2_ref, sem).wait()
    route = route_ref[...]
    w1 = route[:, 2:3]
    w2 = route[:, 3:4]
    lo1, hi1 = _unpack_pairs(y1_ref[...])
    lo2, hi2 = _unpack_pairs(y2_ref[...])
    moe = jnp.concatenate([w1 * lo1 + w2 * lo2, w1 * hi1 + w2 * hi2], axis=1)
    x = x_ref[...] + moe
    xn = _rms(x, g_ref[...]).astype(BF16)
    gate = jax.nn.sigmoid(jnp.dot(xn, wg_ref[...], preferred_element_type=F32))
    emb = jnp.dot(p_ref[...].astype(BF16), wp_ref[...], preferred_element_type=F32)
    out_ref[...] = x + emb * gate


def _moe_ple(pos, ys, x, route, p, ln, wp, wg):
    n = x.shape[0]
    tm = pos.shape[2]
    row = lambda i: (i, 0)
    const = lambda i: (0, 0)
    return pl.pallas_call(
        _moe_ple_body,
        out_shape=jax.ShapeDtypeStruct((n, D_MODEL), F32),
        grid=(n // tm,),
        in_specs=[
            pl.BlockSpec((None, 8, tm), lambda i: (i, 0, 0), memory_space=pltpu.SMEM),
            pl.BlockSpec(memory_space=pl.ANY),
            pl.BlockSpec((tm, D_MODEL), row),
            pl.BlockSpec((tm, LANES), row),
            pl.BlockSpec((tm, PLE_DIM), row),
            pl.BlockSpec((1, D_MODEL), const),
            pl.BlockSpec(wp.shape, const),
            pl.BlockSpec(wg.shape, const),
        ],
        out_specs=pl.BlockSpec((tm, D_MODEL), row),
        scratch_shapes=[pltpu.VMEM((tm, D_MODEL // 2), jnp.uint32), pltpu.VMEM((tm, D_MODEL // 2), jnp.uint32),
                        pltpu.SemaphoreType.DMA(())],
        compiler_params=_cparams(("arbitrary",)),
        name="moe_combine_embed",
    )(pos, ys, x, route, p, ln, wp, wg)


def _rope_tables(pos, dim):
    inv = ROPE_THETA ** (-jnp.arange(0, dim, 2, dtype=F32) / dim)
    ang = pos.astype(F32)[:, None] * inv[None, :]
    ang = jnp.concatenate([ang, ang], axis=-1)
    return jnp.cos(ang), jnp.sin(ang)


def _tile_heads(t):
    return jnp.tile(t, (1, HEADS_PER_BLOCK))


def _even_rope(seq):
    cos, sin = _rope_tables(jnp.arange(seq), HEAD_DIM)
    sign = jnp.where(jnp.arange(HEAD_DIM) < HEAD_DIM // 2, -1.0, 1.0).astype(F32)
    return _tile_heads(cos), _tile_heads(sin * sign)


def _odd_rope(seq):
    t = jnp.arange(seq)
    half = HEAD_DIM // 2
    cr, sr = _rope_tables(t // GRID_W, half)
    cc, sc = _rope_tables(t % GRID_W, half)
    sign = jnp.where(jnp.arange(half) < half // 2, -1.0, 1.0).astype(F32)
    cos = jnp.concatenate([cr, cc], axis=-1)
    sin = jnp.concatenate([sr * sign, sc * sign], axis=-1)
    return _tile_heads(cos), _tile_heads(sin)


def _head_mean_matrix():
    blk = np.arange(HEAD_BLOCK) // HEAD_DIM
    return jnp.asarray((blk[:, None] == blk[None, :]).astype(np.float32) / HEAD_DIM, dtype=BF16)


def _q_head_permutation():
    n_blocks = C_Q_WIDTH // HEAD_BLOCK
    heads = [HEADS_PER_BLOCK * g + c for c in range(n_blocks) for g in range(HEADS_PER_BLOCK)]
    return np.concatenate([np.arange(h * HEAD_DIM, (h + 1) * HEAD_DIM) for h in heads])


def _row(v):
    return v.astype(F32).reshape(1, -1)


def _even_layer(x, batch, seq, w):
    tm = 512
    cos, sin = _even_rope(seq)
    qkv, u = _proj_even(x, w['ln_mix'], w['w_in'], w['bd'], w['qn'], w['kn'], cos, sin, seq, tm)
    outs, lses = [], []
    for g in range(len(A_GROUPS)):
        o, lse = _dilated_attention(qkv, batch, seq, g)
        outs.append(o)
        lses.append(lse)
    ssm_tm = 128
    ys = []
    for direction in range(2):
        tabs = _ssm_tables(*(w[k][direction] for k in ('lam_re', 'lam_im', 'log_dt', 'b_re', 'b_im', 'c_re', 'c_im')),
                           ssm_tm, direction == 1)
        ys.append(_ssm_scan(u, *tabs, batch, seq, ssm_tm, direction == 1))
    x = _even_out(x, outs, lses, u, ys[0], ys[1], w['d_skip'], w['w_glu'], w['w_out'], tm)
    return _ffn(x, w['ln_ffn'], w['ffn_gate'], w['ffn_up'], w['ffn_down'], 256)


def _odd_layer(x, p, batch, seq, w, ple):
    tm = 512
    cos, sin = _odd_rope(seq)
    qt, k, vt = _proj_odd(x, w['ln_mix'], w['w_in'], w['bd'], w['qn'], w['kn'], cos, sin, seq, tm)
    attn = _gqa_attention(qt, k, vt, batch, seq, 256, tm, tm)
    x, xn, route, route_t = _odd_out(x, attn, w['w_out'], w['ln_ffn'], w['router_hi'], w['router_lo'],
                                     MOE_TOKEN_TILE)
    pos, ys = _moe_routed(xn, route_t, w['moe_gate'], w['moe_up'], w['moe_down'])
    return _moe_ple(pos, ys, x, route, p, *ple)


def _prepare_even(j, ln_mix_e, w_in_e, a_qnorm, a_knorm, lam_re, lam_im, log_dt, b_re, b_im, c_re, c_im,
                  ssm_d, ssm_w_glu, w_out_e, ln_ffn_e, ffn_w_gate, ffn_w_up, ffn_w_down):
    return dict(
        ln_mix=_row(ln_mix_e[j]), w_in=w_in_e[j].astype(BF16), bd=_head_mean_matrix(),
        qn=_tile_heads(_row(a_qnorm[j])), kn=_tile_heads(_row(a_knorm[j])),
        lam_re=lam_re[j], lam_im=lam_im[j], log_dt=log_dt[j], b_re=b_re[j], b_im=b_im[j],
        c_re=c_re[j], c_im=c_im[j], d_skip=_row(ssm_d[j]), w_glu=ssm_w_glu[j].astype(BF16),
        w_out=w_out_e[j].astype(BF16), ln_ffn=_row(ln_ffn_e[j]), ffn_gate=ffn_w_gate[j].astype(BF16),
        ffn_up=ffn_w_up[j].astype(BF16), ffn_down=ffn_w_down[j].astype(BF16))


def _prepare_odd(j, ln_mix_o, w_in_o, c_qnorm, c_knorm, w_out_o, ln_ffn_o, router_w, moe_w_gate, moe_w_up,
                 moe_w_down):
    perm = _q_head_permutation()
    w_in = w_in_o[j]
    w_in = jnp.concatenate([w_in[:, :C_Q_WIDTH][:, perm], w_in[:, C_Q_WIDTH:]], axis=1).astype(BF16)
    rw = jnp.pad(router_w[j].astype(F32), ((0, 0), (0, LANES - N_EXPERTS)))
    rw_hi = rw.astype(BF16)
    rw_lo = (rw - rw_hi.astype(F32)).astype(BF16)
    return dict(
        ln_mix=_row(ln_mix_o[j]), w_in=w_in, bd=_head_mean_matrix(),
        qn=_tile_heads(_row(c_qnorm[j])), kn=_tile_heads(_row(c_knorm[j])),
        w_out=w_out_o[j][perm, :].astype(BF16), ln_ffn=_row(ln_ffn_o[j]), router_hi=rw_hi, router_lo=rw_lo,
        moe_gate=moe_w_gate[j].astype(BF16), moe_up=moe_w_up[j].astype(BF16), moe_down=moe_w_down[j].astype(BF16))


def _trunk(x, p, layers, ple):
    batch, seq, _ = x.shape
    x = x.reshape(batch * seq, D_MODEL)
    for i, w in enumerate(layers):
        pi = p[i].reshape(batch * seq, PLE_DIM)
        if i % 2 == 0:
            x = _ple(_even_layer(x, batch, seq, w), pi, *ple[i], 512)
        else:
            x = _odd_layer(x, pi, batch, seq, w, ple[i])
    return x.reshape(batch, seq, D_MODEL)


def kernel(x_prompt, x_sample, p_prompt, p_sample, ln_mix_e, w_in_e, a_qnorm, a_knorm, ssm_lam_re, ssm_lam_im, ssm_log_dt, ssm_b_re, ssm_b_im, ssm_c_re, ssm_c_im, ssm_d, ssm_w_glu, w_out_e, ln_ffn_e, ffn_w_gate, ffn_w_up, ffn_w_down, ln_mix_o, w_in_o, c_qnorm, c_knorm, w_out_o, ln_ffn_o, router_w, moe_w_gate, moe_w_up, moe_w_down, ple_ln, ple_w_proj, ple_w_gate):
    depth = p_prompt.shape[0]
    layers = []
    for i in range(depth):
        j = i // 2
        if i % 2 == 0:
            layers.append(_prepare_even(j, ln_mix_e, w_in_e, a_qnorm, a_knorm, ssm_lam_re, ssm_lam_im, ssm_log_dt,
                                        ssm_b_re, ssm_b_im, ssm_c_re, ssm_c_im, ssm_d, ssm_w_glu, w_out_e, ln_ffn_e,
                                        ffn_w_gate, ffn_w_up, ffn_w_down))
        else:
            layers.append(_prepare_odd(j, ln_mix_o, w_in_o, c_qnorm, c_knorm, w_out_o, ln_ffn_o, router_w,
                                       moe_w_gate, moe_w_up, moe_w_down))
    ple = [(_row(ple_ln[i]), ple_w_proj[i].astype(BF16), ple_w_gate[i].astype(BF16)) for i in range(depth)]
    y_prompt = _trunk(x_prompt, p_prompt, layers, ple)
    y_sample = _trunk(x_sample, p_sample, layers, ple)
    return (y_prompt, y_sample)
```

```python
import functools
import math

import jax
import jax.numpy as jnp
import numpy as np
from jax import lax
from jax.experimental import pallas as pl
from jax.experimental.pallas import tpu as pltpu

F32 = jnp.float32
BF16 = jnp.bfloat16

D_MODEL = 1024
HEAD_DIM = 64
EPS = 1e-6
ROPE_THETA = 10000.0
NEG_INF = -1e30
LOG2_E = math.log2(math.e)
LANES = 128
HEAD_BLOCK = 256
HEADS_PER_BLOCK = HEAD_BLOCK // HEAD_DIM
A_GROUPS = ((128, 1), (512, 4), (2048, 16))
A_HALF = 64
DILATED_ROWS_PER_STEP = 512
A_WIDTH = 768
B_WIDTH = 256
B_GROUP_CH = 16
B_GROUPS = 16
B_STATE = 64
SSM_N = B_GROUPS * B_STATE
SSM_TIME_TILE = 256
QKV_E = 3 * A_WIDTH
D_FF = 2816
GRID_W = 64
C_Q_WIDTH = 1024
C_KV_WIDTH = 256
V_ONES_ROWS = 16
V_HEAD_ROWS = HEAD_DIM + V_ONES_ROWS
N_EXPERTS = 8
TOP_K = 2
D_EXPERT = 3584
MOE_TOKEN_TILE = 512
MOE_ROW_TILE = 512
MOE_FF_CHUNK = 896
PLE_DIM = 256

VMEM_LIMIT = 56 * 1024 * 1024


def _cparams(sem):
    return pltpu.CompilerParams(dimension_semantics=sem, vmem_limit_bytes=VMEM_LIMIT)


def _rms(x, g):
    ms = jnp.mean(x * x, axis=-1, keepdims=True)
    return x * lax.rsqrt(ms + EPS) * g


def _headnorm_rope(y, bd, gain, cos, sin_signed, half):
    ss = jnp.dot((y * y).astype(BF16), bd, preferred_element_type=F32)
    yn = y * lax.rsqrt(ss + EPS) * gain
    width = yn.shape[1]
    lane = lax.broadcasted_iota(jnp.int32, yn.shape, 1)
    up = pltpu.roll(yn, width - half, 1)
    dn = pltpu.roll(yn, half, 1)
    rot = jnp.where((lane & (2 * half - 1)) < half, up, dn)
    return yn * cos + rot * sin_signed


def _proj_even_body(x_ref, g_ref, w_ref, bd_ref, qn_ref, kn_ref, cos_ref, sin_ref, perm1_ref, perm2_ref,
                    qkv0_ref, qkv1_ref, qkv2_ref, u_ref):
    xn = _rms(x_ref[...], g_ref[...]).astype(BF16)
    n_groups = A_WIDTH // HEAD_BLOCK
    outs = (qkv0_ref, qkv1_ref, qkv2_ref)
    perms = (None, perm1_ref, perm2_ref)
    for c in range(QKV_E // HEAD_BLOCK + 1):
        y = jnp.dot(xn, w_ref[:, c * HEAD_BLOCK:(c + 1) * HEAD_BLOCK], preferred_element_type=F32)
        if c == QKV_E // HEAD_BLOCK:
            u_ref[...] = y
            continue
        part, group = divmod(c, n_groups)
        if part < 2:
            gain = qn_ref[...] if part == 0 else kn_ref[...]
            y = _headnorm_rope(y, bd_ref[...], gain, cos_ref[...], sin_ref[...], HEAD_DIM // 2)
            if part == 0:
                y = y * (HEAD_DIM ** -0.5)
        y = y.astype(BF16)
        if perms[group] is not None:
            y = jnp.dot(perms[group][...], y, preferred_element_type=F32).astype(BF16)
        out = outs[group]
        dil, rows = out.shape[0], out.shape[1]
        for r in range(dil):
            out[r, :, part * HEAD_BLOCK:(part + 1) * HEAD_BLOCK] = y[r * rows:(r + 1) * rows, :]


def _residue_permutation(tm, dil):
    dst = np.arange(tm)
    src = (dst % (tm // dil)) * dil + dst // (tm // dil)
    return jnp.asarray((src[:, None] == np.arange(tm)[None, :]).astype(np.float32), dtype=BF16)


def _proj_even(x, ln, w, bd, qn, kn, cos, sin, batch, seq, tm):
    n = x.shape[0]
    n_seq_tiles = seq // tm
    const = lambda i: (0, 0)
    perms = [_residue_permutation(tm, dil) for _, dil in A_GROUPS[1:]]
    qkv_shapes = [jax.ShapeDtypeStruct((batch, dil, seq // dil, 3 * HEAD_BLOCK), BF16) for _, dil in A_GROUPS]
    qkv_specs = [pl.BlockSpec((None, dil, tm // dil, 3 * HEAD_BLOCK),
                              lambda i: (i // n_seq_tiles, 0, i % n_seq_tiles, 0)) for _, dil in A_GROUPS]
    return pl.pallas_call(
        _proj_even_body,
        out_shape=(*qkv_shapes, jax.ShapeDtypeStruct((n, B_WIDTH), F32)),
        grid=(n // tm,),
        in_specs=[
            pl.BlockSpec((tm, D_MODEL), lambda i: (i, 0)),
            pl.BlockSpec((1, D_MODEL), const),
            pl.BlockSpec(w.shape, const),
            pl.BlockSpec(bd.shape, const),
            pl.BlockSpec((1, HEAD_BLOCK), const),
            pl.BlockSpec((1, HEAD_BLOCK), const),
            pl.BlockSpec((tm, HEAD_BLOCK), lambda i: (i % n_seq_tiles, 0)),
            pl.BlockSpec((tm, HEAD_BLOCK), lambda i: (i % n_seq_tiles, 0)),
            pl.BlockSpec((tm, tm), const),
            pl.BlockSpec((tm, tm), const),
        ],
        out_specs=(*qkv_specs, pl.BlockSpec((tm, B_WIDTH), lambda i: (i, 0))),
        compiler_params=_cparams(("arbitrary",)),
        name="proj_even",
    )(x, ln, w, bd, qn, kn, cos, sin, *perms)


def _proj_odd_body(x_ref, g_ref, w_ref, bd_ref, qn_ref, kn_ref, cos_ref, sin_ref, qt_ref, k_ref, vt_ref):
    xn = _rms(x_ref[...], g_ref[...]).astype(BF16)
    tm = xn.shape[0]
    n_q_blocks = C_Q_WIDTH // HEAD_BLOCK
    for c in range(n_q_blocks + 2):
        y = jnp.dot(xn, w_ref[:, c * HEAD_BLOCK:(c + 1) * HEAD_BLOCK], preferred_element_type=F32)
        if c <= n_q_blocks:
            gain = qn_ref[...] if c < n_q_blocks else kn_ref[...]
            y = _headnorm_rope(y, bd_ref[...], gain, cos_ref[...], sin_ref[...], HEAD_DIM // 4)
        if c < n_q_blocks:
            qt_ref[c * HEAD_BLOCK:(c + 1) * HEAD_BLOCK, :] = (y * (HEAD_DIM ** -0.5 * LOG2_E)).T.astype(BF16)
        elif c == n_q_blocks:
            k_ref[...] = y.astype(BF16)
        else:
            yt = y.T.astype(BF16)
            ones = jnp.ones((V_ONES_ROWS, tm), BF16)
            for g in range(HEADS_PER_BLOCK):
                vt_ref[g * V_HEAD_ROWS:g * V_HEAD_ROWS + HEAD_DIM, :] = yt[g * HEAD_DIM:(g + 1) * HEAD_DIM, :]
                vt_ref[g * V_HEAD_ROWS + HEAD_DIM:(g + 1) * V_HEAD_ROWS, :] = ones


def _proj_odd(x, ln, w, bd, qn, kn, cos, sin, seq, tm):
    n = x.shape[0]
    n_seq_tiles = seq // tm
    const = lambda i: (0, 0)
    return pl.pallas_call(
        _proj_odd_body,
        out_shape=(jax.ShapeDtypeStruct((n // tm, C_Q_WIDTH, tm), BF16),
                   jax.ShapeDtypeStruct((n, C_KV_WIDTH), BF16),
                   jax.ShapeDtypeStruct((n // tm, HEADS_PER_BLOCK * V_HEAD_ROWS, tm), BF16)),
        grid=(n // tm,),
        in_specs=[
            pl.BlockSpec((tm, D_MODEL), lambda i: (i, 0)),
            pl.BlockSpec((1, D_MODEL), const),
            pl.BlockSpec(w.shape, const),
            pl.BlockSpec(bd.shape, const),
            pl.BlockSpec((1, HEAD_BLOCK), const),
            pl.BlockSpec((1, HEAD_BLOCK), const),
            pl.BlockSpec((tm, HEAD_BLOCK), lambda i: (i % n_seq_tiles, 0)),
            pl.BlockSpec((tm, HEAD_BLOCK), lambda i: (i % n_seq_tiles, 0)),
        ],
        out_specs=(pl.BlockSpec((None, C_Q_WIDTH, tm), lambda i: (i, 0, 0)),
                   pl.BlockSpec((tm, C_KV_WIDTH), lambda i: (i, 0)),
                   pl.BlockSpec((None, HEADS_PER_BLOCK * V_HEAD_ROWS, tm), lambda i: (i, 0, 0))),
        compiler_params=_cparams(("arbitrary",)),
        name="proj_odd",
    )(x, ln, w, bd, qn, kn, cos, sin)


def _dilated_body(q_ref, k_ref, v_ref, o_ref, lse_ref, *, tq, kw, sub_len):
    n_res, rows = q_ref.shape[0], q_ref.shape[1]
    base = pl.program_id(2) * rows
    lane_head = lax.broadcasted_iota(jnp.int32, (tq, HEAD_BLOCK), 1) // HEAD_DIM
    blocks = [(r, j) for r in range(n_res) for j in range(rows // tq)]
    heads = range(HEADS_PER_BLOCK)
    scores, windows = [], []
    for r, j in blocks:
        q0 = base + j * tq
        k0 = pl.multiple_of(jnp.clip(q0 - A_HALF, 0, sub_len - kw), A_HALF)
        q = q_ref[r, j * tq:(j + 1) * tq, :]
        kwin = k_ref[r, pl.ds(k0, kw), :]
        qpos = q0 + lax.broadcasted_iota(jnp.int32, (tq, kw), 0)
        kpos = k0 + lax.broadcasted_iota(jnp.int32, (tq, kw), 1)
        valid = jnp.abs(kpos - qpos) <= A_HALF
        windows.append(k0)
        for h in heads:
            qh = jnp.where(lane_head == h, q, jnp.zeros_like(q))
            s = lax.dot_general(qh, kwin, (((1,), (1,)), ((), ())), preferred_element_type=F32)
            scores.append(jnp.where(valid, s, NEG_INF))
    probs, lses = [], []
    for s in scores:
        m = jnp.max(s, axis=-1, keepdims=True)
        e = jnp.exp(s - m)
        den = jnp.sum(e, axis=-1, keepdims=True)
        probs.append((e / den).astype(BF16))
        lses.append(m + jnp.log(den))
    for b, (r, j) in enumerate(blocks):
        vwin = v_ref[r, pl.ds(windows[b], kw), :]
        o = jnp.zeros((tq, HEAD_BLOCK), F32)
        lse = jnp.zeros((tq, HEAD_BLOCK), F32)
        for h in heads:
            sel = lane_head == h
            o = jnp.where(sel, jnp.dot(probs[b * HEADS_PER_BLOCK + h], vwin, preferred_element_type=F32), o)
            lse = jnp.where(sel, lses[b * HEADS_PER_BLOCK + h], lse)
        o_ref[r, j * tq:(j + 1) * tq, :] = o.astype(BF16)
        lse_ref[r, j * tq:(j + 1) * tq, :] = lse


def _dilated_attention(qkv, group):
    batch, dil, sub_len, _ = qkv.shape
    tq = min(128, sub_len)
    kw = min(256, sub_len)
    rows = min(DILATED_ROWS_PER_STEP, sub_len)
    n_res = min(dil, DILATED_ROWS_PER_STEP // rows)
    qmap = lambda b, r, i: (b, r, i, 0)
    return pl.pallas_call(
        functools.partial(_dilated_body, tq=tq, kw=kw, sub_len=sub_len),
        out_shape=(jax.ShapeDtypeStruct((batch, dil, sub_len, HEAD_BLOCK), BF16),
                   jax.ShapeDtypeStruct((batch, dil, sub_len, HEAD_BLOCK), F32)),
        grid=(batch, dil // n_res, sub_len // rows),
        in_specs=[
            pl.BlockSpec((None, n_res, rows, HEAD_BLOCK), qmap),
            pl.BlockSpec((None, n_res, sub_len, HEAD_BLOCK), lambda b, r, i: (b, r, 0, 1)),
            pl.BlockSpec((None, n_res, sub_len, HEAD_BLOCK), lambda b, r, i: (b, r, 0, 2)),
        ],
        out_specs=(pl.BlockSpec((None, n_res, rows, HEAD_BLOCK), qmap),
                   pl.BlockSpec((None, n_res, rows, HEAD_BLOCK), qmap)),
        compiler_params=_cparams(("arbitrary", "arbitrary", "arbitrary")),
        name=f"dilated_attn_g{group}",
    )(qkv, qkv, qkv)


def _ssm_body(uf_ref, ub_ref, bmat_ref, a_ref, cmat_ref, yf_ref, yb_ref, carry_ref, *state_refs, tm, pitch):
    n_slabs = SSM_N // LANES
    nb = uf_ref.shape[0]
    chains = [(d, b) for d in range(2) for b in range(nb)]

    @pl.when(pl.program_id(1) == 0)
    def _():
        carry_ref[...] = jnp.zeros_like(carry_ref)

    for c, (d, b) in enumerate(chains):
        u = (uf_ref if d == 0 else ub_ref)[b].astype(BF16)
        bu = jnp.dot(u, bmat_ref[d], preferred_element_type=F32)
        for part in range(2):
            for j in range(n_slabs):
                col = part * SSM_N + j * LANES
                state_refs[2 * c + part][j * pitch:j * pitch + tm, :] = bu[:, col:col + LANES]

    coef = [(a_ref[d, 0], a_ref[d, 1]) for d in range(2)]

    def step(t, hs):
        out = []
        for c, (d, b) in enumerate(chains):
            rows = pl.ds(t if d == 0 else tm - 1 - t, n_slabs, stride=pitch)
            ar, ai = coef[d]
            hr, hi = hs[2 * c], hs[2 * c + 1]
            nr = ar * hr - ai * hi + state_refs[2 * c][rows, :]
            ni = ar * hi + ai * hr + state_refs[2 * c + 1][rows, :]
            state_refs[2 * c][rows, :] = nr
            state_refs[2 * c + 1][rows, :] = ni
            out += [nr, ni]
        return tuple(out)

    final = lax.fori_loop(0, tm, step, tuple(carry_ref[k] for k in range(2 * len(chains))), unroll=8)
    for k, v in enumerate(final):
        carry_ref[k] = v

    for c, (d, b) in enumerate(chains):
        y = jnp.zeros((tm, B_WIDTH), F32)
        for part in range(2):
            for j in range(n_slabs):
                h = state_refs[2 * c + part][j * pitch:j * pitch + tm, :].astype(BF16)
                row0 = part * SSM_N + j * LANES
                y = y + jnp.dot(h, cmat_ref[d, row0:row0 + LANES, :], preferred_element_type=F32)
        (yf_ref if d == 0 else yb_ref)[b] = y


def _ssm_scan(u, bmat, coef, cmat, batch, seq):
    tm = SSM_TIME_TILE
    pitch = tm + 8
    nb = 2 if batch % 2 == 0 else 1
    n_t = seq // tm
    n_chains = 2 * nb
    fwd = lambda b, i: (b, i, 0)
    bwd = lambda b, i: (b, n_t - 1 - i, 0)
    const = lambda b, i: (0,) * 3
    u3 = u.reshape(batch, seq, B_WIDTH)
    yf, yb = pl.pallas_call(
        functools.partial(_ssm_body, tm=tm, pitch=pitch),
        out_shape=(jax.ShapeDtypeStruct((batch, seq, B_WIDTH), F32),) * 2,
        grid=(batch // nb, n_t),
        in_specs=[
            pl.BlockSpec((nb, tm, B_WIDTH), fwd),
            pl.BlockSpec((nb, tm, B_WIDTH), bwd),
            pl.BlockSpec(bmat.shape, const),
            pl.BlockSpec(coef.shape, lambda b, i: (0,) * 4),
            pl.BlockSpec(cmat.shape, const),
        ],
        out_specs=(pl.BlockSpec((nb, tm, B_WIDTH), fwd), pl.BlockSpec((nb, tm, B_WIDTH), bwd)),
        scratch_shapes=[pltpu.VMEM((2 * n_chains, 8, LANES), F32)]
        + [pltpu.VMEM((SSM_N // LANES * pitch, LANES), F32) for _ in range(2 * n_chains)],
        compiler_params=_cparams(("arbitrary", "arbitrary")),
        name="ssm_scan",
    )(u3, u3, bmat, coef, cmat)
    return yf.reshape(batch * seq, B_WIDTH), yb.reshape(batch * seq, B_WIDTH)


def _ssm_tables(lam_re, lam_im, log_dt, b_re, b_im, c_re, c_im):
    lr, li = lam_re.astype(F32), lam_im.astype(F32)
    dt = jnp.exp(log_dt.astype(F32))[:, None]
    mag = jnp.exp(lr * dt)
    a_r = mag * jnp.cos(li * dt)
    a_i = mag * jnp.sin(li * dt)
    den = lr * lr + li * li
    f_r = ((a_r - 1.0) * lr + a_i * li) / den
    f_i = (a_i * lr - (a_r - 1.0) * li) / den
    br, bi = b_re.astype(F32), b_im.astype(F32)
    bb_r = f_r[..., None] * br - f_i[..., None] * bi
    bb_i = f_r[..., None] * bi + f_i[..., None] * br
    eye = jnp.eye(B_GROUPS, dtype=F32)

    def in_mat(bb):
        return jnp.einsum('gpc,gh->gchp', bb, eye).reshape(B_WIDTH, SSM_N)

    def out_mat(c):
        return jnp.einsum('gcp,gh->gphc', c.astype(F32), eye).reshape(SSM_N, B_WIDTH)

    bmat = jnp.concatenate([in_mat(bb_r), in_mat(bb_i)], axis=1).astype(BF16)
    cmat = jnp.concatenate([out_mat(c_re), -out_mat(c_im)], axis=0).astype(BF16)

    coef = jnp.stack([a_r.reshape(SSM_N // LANES, LANES), a_i.reshape(SSM_N // LANES, LANES)])
    return bmat, coef, cmat


def _gelu_tanh(x):
    return 0.5 * x * (1.0 + jnp.tanh(math.sqrt(2.0 / math.pi) * (x + 0.044715 * (x * x * x))))


def _token_order(ref, unperm_ref, split):
    tm = ref.shape[0] * ref.shape[1]
    v = ref[...].reshape(tm, ref.shape[2])
    if unperm_ref is None:
        return v.astype(F32)
    if not split:
        return jnp.dot(unperm_ref[...], v, preferred_element_type=F32)
    hi = v.astype(BF16)
    lo = (v - hi.astype(F32)).astype(BF16)
    return (jnp.dot(unperm_ref[...], hi, preferred_element_type=F32)
            + jnp.dot(unperm_ref[...], lo, preferred_element_type=F32))


def _even_out_body(x_ref, o0_ref, o1_ref, o2_ref, l0_ref, l1_ref, l2_ref, u_ref, yf_ref, yb_ref,
                   d_ref, wglu_ref, wout_ref, unperm1_ref, unperm2_ref, out_ref):
    unperms = (None, unperm1_ref, unperm2_ref)
    l0, l1, l2 = (_token_order(r, p, True) for r, p in zip((l0_ref, l1_ref, l2_ref), unperms))
    o0, o1, o2 = (_token_order(r, p, False) for r, p in zip((o0_ref, o1_ref, o2_ref), unperms))
    lmax = jnp.maximum(jnp.maximum(l0, l1), l2)
    e0, e1, e2 = jnp.exp(l0 - lmax), jnp.exp(l1 - lmax), jnp.exp(l2 - lmax)
    den = e0 + e1 + e2
    a = (e0 / den) * o0 + (e1 / den) * o1 + (e2 / den) * o2
    y = u_ref[...] * d_ref[...] + yf_ref[...] + yb_ref[...]
    z = _gelu_tanh(y)
    z = z * jax.nn.sigmoid(jnp.dot(z.astype(BF16), wglu_ref[...], preferred_element_type=F32))
    acc = jnp.dot(a.astype(BF16), wout_ref[0:HEAD_BLOCK, :], preferred_element_type=F32)
    acc = acc + jnp.dot(z.astype(BF16), wout_ref[HEAD_BLOCK:, :], preferred_element_type=F32)
    out_ref[...] = x_ref[...] + acc


def _even_out(x, os_, ls_, u, yf, yb, d_skip, wglu, wout, seq, tm):
    n = x.shape[0]
    n_seq_tiles = seq // tm
    row = lambda i: (i, 0)
    const = lambda i: (0, 0)
    narrow = pl.BlockSpec((tm, HEAD_BLOCK), row)
    grouped = [pl.BlockSpec((None, dil, tm // dil, HEAD_BLOCK), lambda i: (i // n_seq_tiles, 0, i % n_seq_tiles, 0))
               for _, dil in A_GROUPS]
    unperms = [_residue_permutation(tm, dil).T for _, dil in A_GROUPS[1:]]
    return pl.pallas_call(
        _even_out_body,
        out_shape=jax.ShapeDtypeStruct((n, D_MODEL), F32),
        grid=(n // tm,),
        in_specs=[pl.BlockSpec((tm, D_MODEL), row)] + grouped + grouped + [narrow] * 3 + [
            pl.BlockSpec((1, B_WIDTH), const),
            pl.BlockSpec(wglu.shape, const),
            pl.BlockSpec(wout.shape, const),
            pl.BlockSpec((tm, tm), const),
            pl.BlockSpec((tm, tm), const),
        ],
        out_specs=pl.BlockSpec((tm, D_MODEL), row),
        compiler_params=_cparams(("arbitrary",)),
        name="even_out",
    )(x, *os_, *ls_, u, yf, yb, d_skip, wglu, wout, *unperms)


def _ffn_body(x_ref, g_ref, wg_ref, wu_ref, wd_ref, out_ref, *, chunk):
    x = x_ref[...]
    xn = _rms(x, g_ref[...]).astype(BF16)
    acc = jnp.zeros(x.shape, F32)
    for c in range(D_FF // chunk):
        cols = slice(c * chunk, (c + 1) * chunk)
        g = jnp.dot(xn, wg_ref[:, cols], preferred_element_type=F32)
        up = jnp.dot(xn, wu_ref[:, cols], preferred_element_type=F32)
        h = (g * jax.nn.sigmoid(g) * up).astype(BF16)
        acc = acc + jnp.dot(h, wd_ref[cols, :], preferred_element_type=F32)
    out_ref[...] = x + acc


def _ffn(x, ln, wg, wu, wd, tm):
    n = x.shape[0]
    row = lambda i: (i, 0)
    const = lambda i: (0, 0)
    return pl.pallas_call(
        functools.partial(_ffn_body, chunk=256),
        out_shape=jax.ShapeDtypeStruct((n, D_MODEL), F32),
        grid=(n // tm,),
        in_specs=[
            pl.BlockSpec((tm, D_MODEL), row),
            pl.BlockSpec((1, D_MODEL), const),
            pl.BlockSpec(wg.shape, const),
            pl.BlockSpec(wu.shape, const),
            pl.BlockSpec(wd.shape, const),
        ],
        out_specs=pl.BlockSpec((tm, D_MODEL), row),
        compiler_params=_cparams(("arbitrary",)),
        name="ffn_dense",
    )(x, ln, wg, wu, wd)


def _ple_body(x_ref, p_ref, g_ref, wp_ref, wg_ref, out_ref):
    x = x_ref[...]
    xn = _rms(x, g_ref[...]).astype(BF16)
    gate = jax.nn.sigmoid(jnp.dot(xn, wg_ref[...], preferred_element_type=F32))
    emb = jnp.dot(p_ref[...].astype(BF16), wp_ref[...], preferred_element_type=F32)
    out_ref[...] = x + emb * gate


def _ple(x, p, ln, wp, wg, tm):
    n = x.shape[0]
    row = lambda i: (i, 0)
    const = lambda i: (0, 0)
    return pl.pallas_call(
        _ple_body,
        out_shape=jax.ShapeDtypeStruct((n, D_MODEL), F32),
        grid=(n // tm,),
        in_specs=[
            pl.BlockSpec((tm, D_MODEL), row),
            pl.BlockSpec((tm, PLE_DIM), row),
            pl.BlockSpec((1, D_MODEL), const),
            pl.BlockSpec(wp.shape, const),
            pl.BlockSpec(wg.shape, const),
        ],
        out_specs=pl.BlockSpec((tm, D_MODEL), row),
        compiler_params=_cparams(("arbitrary",)),
        name="per_layer_embed",
    )(x, p, ln, wp, wg)


def _gqa_body(qt_ref, k_ref, vt_ref, o_ref, qs_ref, s_ref, m_ref, acc_ref, *, tq, tk, seq):
    n_kv = seq // tk
    for c in range(C_Q_WIDTH // HEAD_BLOCK):
        qs_ref[...] = jnp.zeros(qs_ref.shape, BF16)
        for g in range(HEADS_PER_BLOCK):
            rows = slice(c * HEAD_BLOCK + g * HEAD_DIM, c * HEAD_BLOCK + (g + 1) * HEAD_DIM)
            qs_ref[g * HEAD_DIM:(g + 1) * HEAD_DIM, g * tq:(g + 1) * tq] = qt_ref[rows, :]
        m_ref[...] = jnp.full(m_ref.shape, NEG_INF, F32)
        acc_ref[...] = jnp.zeros(acc_ref.shape, F32)

        def scores(t, slot):
            k0 = pl.multiple_of(t * tk, tk)
            s_ref[slot] = jnp.dot(k_ref[pl.ds(k0, tk), :], qs_ref[...], preferred_element_type=F32)

        def consume(t, slot):
            s = s_ref[slot]
            m_old = m_ref[...]
            m_new = jnp.maximum(m_old, jnp.max(s, axis=0, keepdims=True))
            alpha = jnp.exp2(m_old - m_new)
            p = jnp.exp2(s - m_new).astype(BF16)
            m_ref[...] = m_new
            for g in range(HEADS_PER_BLOCK):
                cols = slice(g * tq, (g + 1) * tq)
                vt = vt_ref[t, g * V_HEAD_ROWS:(g + 1) * V_HEAD_ROWS, :]
                acc_ref[g] = alpha[:, cols] * acc_ref[g] + jnp.dot(vt, p[:, cols], preferred_element_type=F32)

        scores(0, 0)

        def step(u, carry):
            scores(2 * u + 1, 1)
            consume(2 * u, 0)
            scores(jnp.minimum(2 * u + 2, n_kv - 1), 0)
            consume(2 * u + 1, 1)
            return carry

        lax.fori_loop(0, n_kv // 2, step, 0)
        heads = [acc_ref[g, 0:HEAD_DIM, :] / acc_ref[g, HEAD_DIM:HEAD_DIM + 1, :] for g in range(HEADS_PER_BLOCK)]
        o_ref[:, c * HEAD_BLOCK:(c + 1) * HEAD_BLOCK] = jnp.concatenate(heads, axis=0).T.astype(BF16)


def _gqa_attention(qt, k, vt, batch, seq, tq, tk, tm):
    assert tm % tq == 0 and seq % tk == 0
    per_tm = tm // tq
    tiles_per_seq = seq // tm
    return pl.pallas_call(
        functools.partial(_gqa_body, tq=tq, tk=tk, seq=seq),
        out_shape=jax.ShapeDtypeStruct((batch, seq, C_Q_WIDTH), BF16),
        grid=(batch, seq // tq),
        in_specs=[
            pl.BlockSpec((None, C_Q_WIDTH, tq), lambda b, i: (b * tiles_per_seq + i // per_tm, 0, i % per_tm)),
            pl.BlockSpec((None, seq, C_KV_WIDTH), lambda b, i: (b, 0, 0)),
            pl.BlockSpec((seq // tk, HEADS_PER_BLOCK * V_HEAD_ROWS, tk), lambda b, i: (b, 0, 0)),
        ],
        out_specs=pl.BlockSpec((None, tq, C_Q_WIDTH), lambda b, i: (b, i, 0)),
        scratch_shapes=[pltpu.VMEM((HEAD_BLOCK, HEADS_PER_BLOCK * tq), BF16),
                        pltpu.VMEM((2, tk, HEADS_PER_BLOCK * tq), F32),
                        pltpu.VMEM((1, HEADS_PER_BLOCK * tq), F32),
                        pltpu.VMEM((HEADS_PER_BLOCK, V_HEAD_ROWS, tq), F32)],
        compiler_params=_cparams(("arbitrary", "arbitrary")),
        name="gqa_attention",
    )(qt, k.reshape(batch, seq, C_KV_WIDTH), vt).reshape(batch * seq, C_Q_WIDTH)


def _odd_out_body(x_ref, a_ref, wout_ref, g_ref, rw_hi_ref, rw_lo_ref, out_ref, xn_ref, route_ref, route_t_ref):
    x = x_ref[...] + jnp.dot(a_ref[...], wout_ref[...], preferred_element_type=F32)
    out_ref[...] = x
    h = _rms(x, g_ref[...])
    h_hi = h.astype(BF16)
    h_lo = (h - h_hi.astype(F32)).astype(BF16)
    logits = (jnp.dot(h_hi, rw_hi_ref[...], preferred_element_type=F32)
              + jnp.dot(h_hi, rw_lo_ref[...], preferred_element_type=F32)
              + jnp.dot(h_lo, rw_hi_ref[...], preferred_element_type=F32))
    lane = lax.broadcasted_iota(jnp.int32, logits.shape, 1)
    logits = jnp.where(lane < N_EXPERTS, logits, -jnp.inf)
    m1 = jnp.max(logits, axis=-1, keepdims=True)
    i1 = jnp.min(jnp.where(logits == m1, lane, LANES), axis=-1, keepdims=True)
    rest = jnp.where(lane == i1, -jnp.inf, logits)
    m2 = jnp.max(rest, axis=-1, keepdims=True)
    i2 = jnp.min(jnp.where(rest == m2, lane, LANES), axis=-1, keepdims=True)
    e2 = jnp.exp(m2 - m1)
    w1 = 1.0 / (1.0 + e2)
    w2 = e2 / (1.0 + e2)
    rec = jnp.where(lane == 0, i1.astype(F32),
                    jnp.where(lane == 1, i2.astype(F32), jnp.where(lane == 2, w1, jnp.where(lane == 3, w2, 0.0))))
    route_ref[...] = rec
    route_t_ref[...] = rec.T[0:8, :]
    xn_ref[...] = _pack_pairs(h_hi[:, :D_MODEL // 2].astype(F32), h_hi[:, D_MODEL // 2:].astype(F32))


def _odd_out(x, attn, wout, ln, rw_hi, rw_lo, tm):
    n = x.shape[0]
    row = lambda i: (i, 0)
    const = lambda i: (0, 0)
    return pl.pallas_call(
        _odd_out_body,
        out_shape=(jax.ShapeDtypeStruct((n, D_MODEL), F32), jax.ShapeDtypeStruct((n, D_MODEL // 2), jnp.uint32),
                   jax.ShapeDtypeStruct((n, LANES), F32), jax.ShapeDtypeStruct((8, n), F32)),
        grid=(n // tm,),
        in_specs=[
            pl.BlockSpec((tm, D_MODEL), row),
            pl.BlockSpec((tm, C_Q_WIDTH), row),
            pl.BlockSpec(wout.shape, const),
            pl.BlockSpec((1, D_MODEL), const),
            pl.BlockSpec(rw_hi.shape, const),
            pl.BlockSpec(rw_lo.shape, const),
        ],
        out_specs=(pl.BlockSpec((tm, D_MODEL), row), pl.BlockSpec((tm, D_MODEL // 2), row),
                   pl.BlockSpec((tm, LANES), row), pl.BlockSpec((8, tm), lambda i: (0, i))),
        compiler_params=_cparams(("arbitrary",)),
        name="odd_out_router",
    )(x, attn, wout, ln, rw_hi, rw_lo)


def _pack_pairs(lo, hi):
    lo_bits = lax.bitcast_convert_type(lo.astype(BF16).astype(F32), jnp.uint32) >> 16
    hi_bits = lax.bitcast_convert_type(hi.astype(BF16).astype(F32), jnp.uint32) & jnp.uint32(0xFFFF0000)
    return lo_bits | hi_bits


def _unpack_pairs(u):
    lo = lax.bitcast_convert_type(u << 16, F32)
    hi = lax.bitcast_convert_type(u & jnp.uint32(0xFFFF0000), F32)
    return lo, hi


def _rank_body(route_ref, tri_ref, pos_ref, meta_ref, cnt_ref, off_ref, *, tile_rows):
    phase = pl.program_id(0)
    i = pl.program_id(1)
    tm = route_ref.shape[1]
    route = route_ref[...]
    i1 = route[0:1, :]
    i2 = route[1:2, :]
    row = lax.broadcasted_iota(jnp.int32, (N_EXPERTS, tm), 0).astype(F32)
    hit = ((row == i1) | (row == i2)).astype(F32)
    per_expert = jnp.sum(hit, axis=1, keepdims=True)
    erow = lax.broadcasted_iota(jnp.int32, (N_EXPERTS, 1), 0)

    @pl.when((phase == 0) & (i == 0))
    def _():
        cnt_ref[...] = jnp.zeros_like(cnt_ref)

    @pl.when(phase == 0)
    def _():
        cnt_ref[...] += per_expert

    @pl.when((phase == 1) & (i == 0))
    def _():
        tot = cnt_ref[...]
        padded = jnp.floor((tot + (tile_rows - 1)) * (1.0 / tile_rows)) * tile_rows
        off = jnp.zeros_like(tot)
        for e in range(N_EXPERTS - 1):
            pe = jnp.sum(jnp.where(erow == e, padded, 0.0), axis=0, keepdims=True)
            off = off + jnp.where(erow > e, pe, 0.0)
        off_ref[...] = off
        lane = lax.broadcasted_iota(jnp.int32, meta_ref.shape, 1)
        meta_ref[...] = jnp.where(lane == 0, off, jnp.where(lane == 1, tot, 0.0))
        cnt_ref[...] = jnp.zeros_like(cnt_ref)

    @pl.when(phase == 1)
    def _():
        hit16 = jnp.concatenate([hit, jnp.zeros_like(hit)], axis=0).astype(BF16)
        before = jnp.dot(hit16, tri_ref[...], preferred_element_type=F32)[0:N_EXPERTS, :]
        slot_row = off_ref[...] + cnt_ref[...] + before
        pos1 = jnp.sum(jnp.where(row == i1, slot_row, 0.0), axis=0, keepdims=True)
        pos2 = jnp.sum(jnp.where(row == i2, slot_row, 0.0), axis=0, keepdims=True)
        pos_ref[...] = jnp.concatenate([pos1, pos2, jnp.zeros((6, tm), F32)], axis=0).astype(jnp.int32)
        cnt_ref[...] += per_expert


def _rank(route_t, tri, tm, tile_rows):
    n = route_t.shape[1]
    return pl.pallas_call(
        functools.partial(_rank_body, tile_rows=tile_rows),
        out_shape=(jax.ShapeDtypeStruct((n // tm, 8, tm), jnp.int32), jax.ShapeDtypeStruct((N_EXPERTS, LANES), F32)),
        grid=(2, n // tm),
        in_specs=[pl.BlockSpec((8, tm), lambda ph, i: (0, i)), pl.BlockSpec(tri.shape, lambda ph, i: (0, 0))],
        out_specs=(pl.BlockSpec((None, 8, tm), lambda ph, i: (i * ph, 0, 0)),
                   pl.BlockSpec((N_EXPERTS, LANES), lambda ph, i: (0, 0))),
        scratch_shapes=[pltpu.VMEM((N_EXPERTS, 1), F32), pltpu.VMEM((N_EXPERTS, 1), F32)],
        compiler_params=_cparams(("arbitrary", "arbitrary")),
        name="moe_rank",
    )(route_t, tri)


def _row_copy(src, src_row, dst, dst_row, sem):
    return pltpu.make_async_copy(src.at[pl.ds(src_row, 1)], dst.at[pl.ds(dst_row, 1)], sem)


def _dispatch_body(pos_ref, xn_ref, buf_ref, xs_ref, sem, *, tm):
    del buf_ref

    def issue(t, carry):
        _row_copy(xn_ref, t, xs_ref, pos_ref[0, t], sem).start()
        _row_copy(xn_ref, t, xs_ref, pos_ref[1, t], sem).start()
        return carry

    lax.fori_loop(0, tm, issue, 0, unroll=8)
    for _ in range(TOP_K):
        pltpu.make_async_copy(xn_ref, xs_ref.at[pl.ds(0, tm)], sem).wait()


def _dispatch(pos, xn, rows, tm):
    n = xn.shape[0]
    buf = jnp.zeros((rows, xn.shape[1]), xn.dtype)
    return pl.pallas_call(
        functools.partial(_dispatch_body, tm=tm),
        out_shape=jax.ShapeDtypeStruct(buf.shape, buf.dtype),
        grid=(n // tm,),
        in_specs=[pl.BlockSpec((None, 8, tm), lambda i: (i, 0, 0), memory_space=pltpu.SMEM),
                  pl.BlockSpec((tm, xn.shape[1]), lambda i: (i, 0)), pl.BlockSpec(memory_space=pl.ANY)],
        out_specs=pl.BlockSpec(memory_space=pl.ANY),
        scratch_shapes=[pltpu.SemaphoreType.DMA(())],
        input_output_aliases={2: 0},
        compiler_params=_cparams(("arbitrary",)),
        name="moe_dispatch",
    )(pos, xn, buf)


def _expert_ffn_body(te_ref, tv_ref, xs_ref, wg_ref, wu_ref, wd_ref, ys_ref, xa_ref, xb_ref, acc_ref):
    del te_ref
    j = pl.program_id(0)
    c = pl.program_id(1)
    half = D_MODEL // 2

    @pl.when(c == 0)
    def _():
        lo, hi = _unpack_pairs(xs_ref[...])
        xa_ref[...] = lo.astype(BF16)
        xb_ref[...] = hi.astype(BF16)
        acc_ref[...] = jnp.zeros_like(acc_ref)

    @pl.when(tv_ref[j] != 0)
    def _():
        n_split = 2
        rows = xa_ref.shape[0] // n_split
        hidden = []
        for part in range(n_split):
            rs = slice(part * rows, (part + 1) * rows)
            xa, xb = xa_ref[rs, :], xb_ref[rs, :]
            g = (jnp.dot(xa, wg_ref[0:half, :], preferred_element_type=F32)
                 + jnp.dot(xb, wg_ref[half:, :], preferred_element_type=F32))
            up = (jnp.dot(xa, wu_ref[0:half, :], preferred_element_type=F32)
                  + jnp.dot(xb, wu_ref[half:, :], preferred_element_type=F32))
            hidden.append((g, up))
        for part, (g, up) in enumerate(hidden):
            rs = slice(part * rows, (part + 1) * rows)
            h = (g * jax.nn.sigmoid(g) * up).astype(BF16)
            acc_ref[rs, :] += jnp.dot(h, wd_ref[...], preferred_element_type=F32)

    @pl.when(c == pl.num_programs(1) - 1)
    def _():
        acc = acc_ref[...]
        ys_ref[...] = _pack_pairs(acc[:, :half], acc[:, half:])


def _expert_ffn(tile_expert, tile_valid, xs, wg, wu, wd, tile_rows, chunk):
    rows, half = xs.shape
    n_chunks = D_EXPERT // chunk

    def chunk_of(j, c, tv):
        return jnp.where(tv[j] != 0, c, n_chunks - 1)

    grid_spec = pltpu.PrefetchScalarGridSpec(
        num_scalar_prefetch=2,
        grid=(rows // tile_rows, n_chunks),
        in_specs=[
            pl.BlockSpec((tile_rows, half), lambda j, c, te, tv: (j, 0)),
            pl.BlockSpec((None, D_MODEL, chunk), lambda j, c, te, tv: (te[j], 0, chunk_of(j, c, tv))),
            pl.BlockSpec((None, D_MODEL, chunk), lambda j, c, te, tv: (te[j], 0, chunk_of(j, c, tv))),
            pl.BlockSpec((None, chunk, D_MODEL), lambda j, c, te, tv: (te[j], chunk_of(j, c, tv), 0)),
        ],
        out_specs=pl.BlockSpec((tile_rows, half), lambda j, c, te, tv: (j, 0)),
        scratch_shapes=[pltpu.VMEM((tile_rows, half), BF16), pltpu.VMEM((tile_rows, half), BF16),
                        pltpu.VMEM((tile_rows, D_MODEL), F32)],
    )
    return pl.pallas_call(
        _expert_ffn_body,
        out_shape=jax.ShapeDtypeStruct(xs.shape, xs.dtype),
        grid_spec=grid_spec,
        compiler_params=_cparams(("arbitrary", "arbitrary")),
        name="moe_expert_ffn",
    )(tile_expert, tile_valid, xs, wg, wu, wd)


def _moe_routed(xn, route_t, wg, wu, wd):
    n = xn.shape[0]
    tm = MOE_TOKEN_TILE
    tile_rows = MOE_ROW_TILE
    rows = TOP_K * n + N_EXPERTS * tile_rows
    tri = jnp.asarray(np.triu(np.ones((tm, tm), np.float32), k=1), dtype=BF16)
    pos, meta = _rank(route_t, tri, tm, tile_rows)
    seg_end = meta[:, 0] + jnp.ceil(meta[:, 1] / tile_rows) * tile_rows
    tile_start = (jnp.arange(rows // tile_rows) * tile_rows).astype(F32)
    tile_expert = jnp.minimum(jnp.sum(tile_start[:, None] >= seg_end[None, :], axis=1), N_EXPERTS - 1).astype(jnp.int32)
    tile_valid = (tile_start < seg_end[N_EXPERTS - 1]).astype(jnp.int32)
    xs = _dispatch(pos, xn, rows, tm)
    return pos, _expert_ffn(tile_expert, tile_valid, xs, wg, wu, wd, tile_rows, MOE_FF_CHUNK)


def _moe_ple_body(pos_ref, ys_ref, x_ref, route_ref, p_ref, g_ref, wp_ref, wg_ref, out_ref, y1_ref, y2_ref, sem):
    tm = x_ref.shape[0]

    def issue(t, carry):
        _row_copy(ys_ref, pos_ref[0, t], y1_ref, t, sem).start()
        _row_copy(ys_ref, pos_ref[1, t], y2_ref, t, sem).start()
        return carry

    lax.fori_loop(0, tm, issue, 0, unroll=8)
    pltpu.make_async_copy(ys_ref.at[pl.ds(0, tm)], y1_ref, sem).wait()
    pltpu.make_async_copy(ys_ref.at[pl.ds(0, tm)], y2_ref, sem).wait()
    route = route_ref[...]
    w1 = route[:, 2:3]
    w2 = route[:, 3:4]
    lo1, hi1 = _unpack_pairs(y1_ref[...])
    lo2, hi2 = _unpack_pairs(y2_ref[...])
    moe = jnp.concatenate([w1 * lo1 + w2 * lo2, w1 * hi1 + w2 * hi2], axis=1)
    x = x_ref[...] + moe
    xn = _rms(x, g_ref[...]).astype(BF16)
    gate = jax.nn.sigmoid(jnp.dot(xn, wg_ref[...], preferred_element_type=F32))
    emb = jnp.dot(p_ref[...].astype(BF16), wp_ref[...], preferred_element_type=F32)
    out_ref[...] = x + emb * gate


def _moe_ple(pos, ys, x, route, p, ln, wp, wg):
    n = x.shape[0]
    tm = pos.shape[2]
    row = lambda i: (i, 0)
    const = lambda i: (0, 0)
    return pl.pallas_call(
        _moe_ple_body,
        out_shape=jax.ShapeDtypeStruct((n, D_MODEL), F32),
        grid=(n // tm,),
        in_specs=[
            pl.BlockSpec((None, 8, tm), lambda i: (i, 0, 0), memory_space=pltpu.SMEM),
            pl.BlockSpec(memory_space=pl.ANY),
            pl.BlockSpec((tm, D_MODEL), row),
            pl.BlockSpec((tm, LANES), row),
            pl.BlockSpec((tm, PLE_DIM), row),
            pl.BlockSpec((1, D_MODEL), const),
            pl.BlockSpec(wp.shape, const),
            pl.BlockSpec(wg.shape, const),
        ],
        out_specs=pl.BlockSpec((tm, D_MODEL), row),
        scratch_shapes=[pltpu.VMEM((tm, D_MODEL // 2), jnp.uint32), pltpu.VMEM((tm, D_MODEL // 2), jnp.uint32),
                        pltpu.SemaphoreType.DMA(())],
        compiler_params=_cparams(("arbitrary",)),
        name="moe_combine_embed",
    )(pos, ys, x, route, p, ln, wp, wg)


def _rope_tables(pos, dim):
    inv = ROPE_THETA ** (-jnp.arange(0, dim, 2, dtype=F32) / dim)
    ang = pos.astype(F32)[:, None] * inv[None, :]
    ang = jnp.concatenate([ang, ang], axis=-1)
    return jnp.cos(ang), jnp.sin(ang)


def _tile_heads(t):
    return jnp.tile(t, (1, HEADS_PER_BLOCK))


def _even_rope(seq):
    cos, sin = _rope_tables(jnp.arange(seq), HEAD_DIM)
    sign = jnp.where(jnp.arange(HEAD_DIM) < HEAD_DIM // 2, -1.0, 1.0).astype(F32)
    return _tile_heads(cos), _tile_heads(sin * sign)


def _odd_rope(seq):
    t = jnp.arange(seq)
    half = HEAD_DIM // 2
    cr, sr = _rope_tables(t // GRID_W, half)
    cc, sc = _rope_tables(t % GRID_W, half)
    sign = jnp.where(jnp.arange(half) < half // 2, -1.0, 1.0).astype(F32)
    cos = jnp.concatenate([cr, cc], axis=-1)
    sin = jnp.concatenate([sr * sign, sc * sign], axis=-1)
    return _tile_heads(cos), _tile_heads(sin)


def _head_mean_matrix():
    blk = np.arange(HEAD_BLOCK) // HEAD_DIM
    return jnp.asarray((blk[:, None] == blk[None, :]).astype(np.float32) / HEAD_DIM, dtype=BF16)


def _q_head_permutation():
    n_blocks = C_Q_WIDTH // HEAD_BLOCK
    heads = [HEADS_PER_BLOCK * g + c for c in range(n_blocks) for g in range(HEADS_PER_BLOCK)]
    return np.concatenate([np.arange(h * HEAD_DIM, (h + 1) * HEAD_DIM) for h in heads])


def _row(v):
    return v.astype(F32).reshape(1, -1)


def _even_layer(x, batch, seq, w):
    tm = 512
    cos, sin = _even_rope(seq)
    *qkvs, u = _proj_even(x, w['ln_mix'], w['w_in'], w['bd'], w['qn'], w['kn'], cos, sin, batch, seq, tm)
    outs, lses = [], []
    for g, qkv in enumerate(qkvs):
        o, lse = _dilated_attention(qkv, g)
        outs.append(o)
        lses.append(lse)
    tabs = [_ssm_tables(*(w[k][direction] for k in ('lam_re', 'lam_im', 'log_dt', 'b_re', 'b_im', 'c_re', 'c_im')))
            for direction in range(2)]
    ys = _ssm_scan(u, *(jnp.stack(t) for t in zip(*tabs)), batch, seq)
    x = _even_out(x, outs, lses, u, ys[0], ys[1], w['d_skip'], w['w_glu'], w['w_out'], seq, tm)
    return _ffn(x, w['ln_ffn'], w['ffn_gate'], w['ffn_up'], w['ffn_down'], 256)


def _odd_layer(x, p, batch, seq, w, ple):
    tm = 512
    cos, sin = _odd_rope(seq)
    qt, k, vt = _proj_odd(x, w['ln_mix'], w['w_in'], w['bd'], w['qn'], w['kn'], cos, sin, seq, tm)
    attn = _gqa_attention(qt, k, vt, batch, seq, 256, tm, tm)
    x, xn, route, route_t = _odd_out(x, attn, w['w_out'], w['ln_ffn'], w['router_hi'], w['router_lo'],
                                     MOE_TOKEN_TILE)
    pos, ys = _moe_routed(xn, route_t, w['moe_gate'], w['moe_up'], w['moe_down'])
    return _moe_ple(pos, ys, x, route, p, *ple)


def _prepare_even(j, ln_mix_e, w_in_e, a_qnorm, a_knorm, lam_re, lam_im, log_dt, b_re, b_im, c_re, c_im,
                  ssm_d, ssm_w_glu, w_out_e, ln_ffn_e, ffn_w_gate, ffn_w_up, ffn_w_down):
    return dict(
        ln_mix=_row(ln_mix_e[j]), w_in=w_in_e[j].astype(BF16), bd=_head_mean_matrix(),
        qn=_tile_heads(_row(a_qnorm[j])), kn=_tile_heads(_row(a_knorm[j])),
        lam_re=lam_re[j], lam_im=lam_im[j], log_dt=log_dt[j], b_re=b_re[j], b_im=b_im[j],
        c_re=c_re[j], c_im=c_im[j], d_skip=_row(ssm_d[j]), w_glu=ssm_w_glu[j].astype(BF16),
        w_out=w_out_e[j].astype(BF16), ln_ffn=_row(ln_ffn_e[j]), ffn_gate=ffn_w_gate[j].astype(BF16),
        ffn_up=ffn_w_up[j].astype(BF16), ffn_down=ffn_w_down[j].astype(BF16))


def _prepare_odd(j, ln_mix_o, w_in_o, c_qnorm, c_knorm, w_out_o, ln_ffn_o, router_w, moe_w_gate, moe_w_up,
                 moe_w_down):
    perm = _q_head_permutation()
    w_in = w_in_o[j]
    w_in = jnp.concatenate([w_in[:, :C_Q_WIDTH][:, perm], w_in[:, C_Q_WIDTH:]], axis=1).astype(BF16)
    rw = jnp.pad(router_w[j].astype(F32), ((0, 0), (0, LANES - N_EXPERTS)))
    rw_hi = rw.astype(BF16)
    rw_lo = (rw - rw_hi.astype(F32)).astype(BF16)
    return dict(
        ln_mix=_row(ln_mix_o[j]), w_in=w_in, bd=_head_mean_matrix(),
        qn=_tile_heads(_row(c_qnorm[j])), kn=_tile_heads(_row(c_knorm[j])),
        w_out=w_out_o[j][perm, :].astype(BF16), ln_ffn=_row(ln_ffn_o[j]), router_hi=rw_hi, router_lo=rw_lo,
        moe_gate=moe_w_gate[j].astype(BF16), moe_up=moe_w_up[j].astype(BF16), moe_down=moe_w_down[j].astype(BF16))


def _trunk(x, p, layers, ple):
    batch, seq, _ = x.shape
    x = x.reshape(batch * seq, D_MODEL)
    for i, w in enumerate(layers):
        pi = p[i].reshape(batch * seq, PLE_DIM)
        if i % 2 == 0:
            x = _ple(_even_layer(x, batch, seq, w), pi, *ple[i], 512)
        else:
            x = _odd_layer(x, pi, batch, seq, w, ple[i])
    return x.reshape(batch, seq, D_MODEL)


def kernel(x_prompt, x_sample, p_prompt, p_sample, ln_mix_e, w_in_e, a_qnorm, a_knorm, ssm_lam_re, ssm_lam_im, ssm_log_dt, ssm_b_re, ssm_b_im, ssm_c_re, ssm_c_im, ssm_d, ssm_w_glu, w_out_e, ln_ffn_e, ffn_w_gate, ffn_w_up, ffn_w_down, ln_mix_o, w_in_o, c_qnorm, c_knorm, w_out_o, ln_ffn_o, router_w, moe_w_gate, moe_w_up, moe_w_down, ple_ln, ple_w_proj, ple_w_gate):
    depth = p_prompt.shape[0]
    layers = []
    for i in range(depth):
        j = i // 2
        if i % 2 == 0:
            layers.append(_prepare_even(j, ln_mix_e, w_in_e, a_qnorm, a_knorm, ssm_lam_re, ssm_lam_im, ssm_log_dt,
                                        ssm_b_re, ssm_b_im, ssm_c_re, ssm_c_im, ssm_d, ssm_w_glu, w_out_e, ln_ffn_e,
                                        ffn_w_gate, ffn_w_up, ffn_w_down))
        else:
            layers.append(_prepare_odd(j, ln_mix_o, w_in_o, c_qnorm, c_knorm, w_out_o, ln_ffn_o, router_w,
                                       moe_w_gate, moe_w_up, moe_w_down))
    ple = [(_row(ple_ln[i]), ple_w_proj[i].astype(BF16), ple_w_gate[i].astype(BF16)) for i in range(depth)]
    y_prompt = _trunk(x_prompt, p_prompt, layers, ple)
    y_sample = _trunk(x_sample, p_sample, layers, ple)
    return (y_prompt, y_sample)
```

```python
import functools
import math

import jax
import jax.numpy as jnp
import numpy as np
from jax import lax
from jax.experimental import pallas as pl
from jax.experimental.pallas import tpu as pltpu

F32 = jnp.float32
BF16 = jnp.bfloat16

D_MODEL = 1024
HEAD_DIM = 64
EPS = 1e-6
ROPE_THETA = 10000.0
NEG_INF = -1e30
LOG2_E = math.log2(math.e)
LANES = 128
HEAD_BLOCK = 256
HEADS_PER_BLOCK = HEAD_BLOCK // HEAD_DIM
A_GROUPS = ((128, 1), (512, 4), (2048, 16))
A_HALF = 64
DILATED_ROWS_PER_STEP = 512
A_WIDTH = 768
B_WIDTH = 256
B_GROUP_CH = 16
B_GROUPS = 16
B_STATE = 64
SSM_N = B_GROUPS * B_STATE
SSM_TIME_TILE = 256
QKV_E = 3 * A_WIDTH
D_FF = 2816
GRID_W = 64
C_Q_WIDTH = 1024
C_KV_WIDTH = 256
V_ONES_ROWS = 16
V_HEAD_ROWS = HEAD_DIM + V_ONES_ROWS
GQA_KV_TILE = 1024
N_EXPERTS = 8
TOP_K = 2
D_EXPERT = 3584
MOE_TOKEN_TILE = 512
MOE_ROW_TILE = 512
MOE_FF_CHUNK = 896
PLE_DIM = 256

VMEM_LIMIT = 56 * 1024 * 1024


def _cparams(sem):
    return pltpu.CompilerParams(dimension_semantics=sem, vmem_limit_bytes=VMEM_LIMIT)


def _rms(x, g):
    ms = jnp.mean(x * x, axis=-1, keepdims=True)
    return x * lax.rsqrt(ms + EPS) * g


def _rope(yn, cos, sin_signed, half):
    width = yn.shape[1]
    lane = lax.broadcasted_iota(jnp.int32, yn.shape, 1)
    up = pltpu.roll(yn, width - half, 1)
    dn = pltpu.roll(yn, half, 1)
    rot = jnp.where((lane & (2 * half - 1)) < half, up, dn)
    return yn * cos + rot * sin_signed


def _proj_even_body(x_ref, g_ref, w_ref, bd_ref, qn_ref, kn_ref, cos_ref, sin_ref, perm1_ref, perm2_ref,
                    qkv0_ref, qkv1_ref, qkv2_ref, u_ref):
    xn = _rms(x_ref[...], g_ref[...]).astype(BF16)
    n_groups = A_WIDTH // HEAD_BLOCK
    outs = (qkv0_ref, qkv1_ref, qkv2_ref)
    perms = (None, perm1_ref, perm2_ref)
    n_blocks = QKV_E // HEAD_BLOCK
    ys = [jnp.dot(xn, w_ref[:, c * HEAD_BLOCK:(c + 1) * HEAD_BLOCK], preferred_element_type=F32)
          for c in range(n_blocks + 1)]
    u_ref[...] = ys[n_blocks]
    sums = [jnp.dot((ys[c] * ys[c]).astype(BF16), bd_ref[...], preferred_element_type=F32)
            for c in range(2 * n_groups)]
    for c in range(2 * n_groups):
        scale = (HEAD_DIM ** -0.5) if c < n_groups else 1.0
        gain = qn_ref[...] if c < n_groups else kn_ref[...]
        ys[c] = _rope(ys[c] * lax.rsqrt(sums[c] + EPS) * gain, cos_ref[...], sin_ref[...], HEAD_DIM // 2) * scale
    ys = [y.astype(BF16) for y in ys[:n_blocks]]
    for c in range(n_blocks):
        group = c % n_groups
        if perms[group] is not None:
            ys[c] = jnp.dot(perms[group][...], ys[c], preferred_element_type=F32).astype(BF16)
    for c in range(n_blocks):
        part, group = divmod(c, n_groups)
        out = outs[group]
        dil, rows = out.shape[0], out.shape[1]
        for r in range(dil):
            out[r, :, part * HEAD_BLOCK:(part + 1) * HEAD_BLOCK] = ys[c][r * rows:(r + 1) * rows, :]


def _residue_permutation(tm, dil):
    dst = np.arange(tm)
    src = (dst % (tm // dil)) * dil + dst // (tm // dil)
    return jnp.asarray((src[:, None] == np.arange(tm)[None, :]).astype(np.float32), dtype=BF16)


def _proj_even(x, ln, w, bd, qn, kn, cos, sin, batch, seq, tm):
    n = x.shape[0]
    n_seq_tiles = seq // tm
    const = lambda i: (0, 0)
    perms = [_residue_permutation(tm, dil) for _, dil in A_GROUPS[1:]]
    qkv_shapes = [jax.ShapeDtypeStruct((batch, dil, seq // dil, 3 * HEAD_BLOCK), BF16) for _, dil in A_GROUPS]
    qkv_specs = [pl.BlockSpec((None, dil, tm // dil, 3 * HEAD_BLOCK),
                              lambda i: (i // n_seq_tiles, 0, i % n_seq_tiles, 0)) for _, dil in A_GROUPS]
    return pl.pallas_call(
        _proj_even_body,
        out_shape=(*qkv_shapes, jax.ShapeDtypeStruct((n, B_WIDTH), F32)),
        grid=(n // tm,),
        in_specs=[
            pl.BlockSpec((tm, D_MODEL), lambda i: (i, 0)),
            pl.BlockSpec((1, D_MODEL), const),
            pl.BlockSpec(w.shape, const),
            pl.BlockSpec(bd.shape, const),
            pl.BlockSpec((1, HEAD_BLOCK), const),
            pl.BlockSpec((1, HEAD_BLOCK), const),
            pl.BlockSpec((tm, HEAD_BLOCK), lambda i: (i % n_seq_tiles, 0)),
            pl.BlockSpec((tm, HEAD_BLOCK), lambda i: (i % n_seq_tiles, 0)),
            pl.BlockSpec((tm, tm), const),
            pl.BlockSpec((tm, tm), const),
        ],
        out_specs=(*qkv_specs, pl.BlockSpec((tm, B_WIDTH), lambda i: (i, 0))),
        compiler_params=_cparams(("arbitrary",)),
        name="proj_even",
    )(x, ln, w, bd, qn, kn, cos, sin, *perms)


def _proj_odd_body(x_ref, g_ref, w_ref, bd_ref, qn_ref, kn_ref, cos_ref, sin_ref, qt_ref, k_ref, vt_ref):
    xn = _rms(x_ref[...], g_ref[...]).astype(BF16)
    tm = xn.shape[0]
    n_q_blocks = C_Q_WIDTH // HEAD_BLOCK
    ys = [jnp.dot(xn, w_ref[:, c * HEAD_BLOCK:(c + 1) * HEAD_BLOCK], preferred_element_type=F32)
          for c in range(n_q_blocks + 2)]
    sums = [jnp.dot((ys[c] * ys[c]).astype(BF16), bd_ref[...], preferred_element_type=F32)
            for c in range(n_q_blocks + 1)]
    for c in range(n_q_blocks + 2):
        y = ys[c]
        if c <= n_q_blocks:
            gain = qn_ref[...] if c < n_q_blocks else kn_ref[...]
            y = _rope(y * lax.rsqrt(sums[c] + EPS) * gain, cos_ref[...], sin_ref[...], HEAD_DIM // 4)
        if c < n_q_blocks:
            qt_ref[c * HEAD_BLOCK:(c + 1) * HEAD_BLOCK, :] = (y * (HEAD_DIM ** -0.5 * LOG2_E)).T.astype(BF16)
        elif c == n_q_blocks:
            k_ref[...] = y.astype(BF16)
        else:
            yt = y.T.astype(BF16)
            ones = jnp.ones((V_ONES_ROWS, tm), BF16)
            for g in range(HEADS_PER_BLOCK):
                vt_ref[g * V_HEAD_ROWS:g * V_HEAD_ROWS + HEAD_DIM, :] = yt[g * HEAD_DIM:(g + 1) * HEAD_DIM, :]
                vt_ref[g * V_HEAD_ROWS + HEAD_DIM:(g + 1) * V_HEAD_ROWS, :] = ones


def _proj_odd(x, ln, w, bd, qn, kn, cos, sin, seq, tm):
    n = x.shape[0]
    n_seq_tiles = seq // tm
    const = lambda i: (0, 0)
    return pl.pallas_call(
        _proj_odd_body,
        out_shape=(jax.ShapeDtypeStruct((n // tm, C_Q_WIDTH, tm), BF16),
                   jax.ShapeDtypeStruct((n, C_KV_WIDTH), BF16),
                   jax.ShapeDtypeStruct((n // tm, HEADS_PER_BLOCK * V_HEAD_ROWS, tm), BF16)),
        grid=(n // tm,),
        in_specs=[
            pl.BlockSpec((tm, D_MODEL), lambda i: (i, 0)),
            pl.BlockSpec((1, D_MODEL), const),
            pl.BlockSpec(w.shape, const),
            pl.BlockSpec(bd.shape, const),
            pl.BlockSpec((1, HEAD_BLOCK), const),
            pl.BlockSpec((1, HEAD_BLOCK), const),
            pl.BlockSpec((tm, HEAD_BLOCK), lambda i: (i % n_seq_tiles, 0)),
            pl.BlockSpec((tm, HEAD_BLOCK), lambda i: (i % n_seq_tiles, 0)),
        ],
        out_specs=(pl.BlockSpec((None, C_Q_WIDTH, tm), lambda i: (i, 0, 0)),
                   pl.BlockSpec((tm, C_KV_WIDTH), lambda i: (i, 0)),
                   pl.BlockSpec((None, HEADS_PER_BLOCK * V_HEAD_ROWS, tm), lambda i: (i, 0, 0))),
        compiler_params=_cparams(("arbitrary",)),
        name="proj_odd",
    )(x, ln, w, bd, qn, kn, cos, sin)


def _dilated_body(q_ref, k_ref, v_ref, o_ref, lse_ref, *, tq, kw, sub_len):
    n_res, rows = q_ref.shape[0], q_ref.shape[1]
    base = pl.program_id(2) * rows
    lane_head = lax.broadcasted_iota(jnp.int32, (tq, HEAD_BLOCK), 1) // HEAD_DIM
    blocks = [(r, j) for r in range(n_res) for j in range(rows // tq)]
    heads = range(HEADS_PER_BLOCK)
    scores, windows = [], []
    for r, j in blocks:
        q0 = base + j * tq
        k0 = pl.multiple_of(jnp.clip(q0 - A_HALF, 0, sub_len - kw), A_HALF)
        q = q_ref[r, j * tq:(j + 1) * tq, :]
        kwin = k_ref[r, pl.ds(k0, kw), :]
        qpos = q0 + lax.broadcasted_iota(jnp.int32, (tq, kw), 0)
        kpos = k0 + lax.broadcasted_iota(jnp.int32, (tq, kw), 1)
        valid = jnp.abs(kpos - qpos) <= A_HALF
        windows.append(k0)
        for h in heads:
            qh = jnp.where(lane_head == h, q, jnp.zeros_like(q))
            s = lax.dot_general(qh, kwin, (((1,), (1,)), ((), ())), preferred_element_type=F32)
            scores.append(jnp.where(valid, s, NEG_INF))
    probs, lses = [], []
    for s in scores:
        m = jnp.max(s, axis=-1, keepdims=True)
        e = jnp.exp(s - m)
        den = jnp.sum(e, axis=-1, keepdims=True)
        probs.append((e / den).astype(BF16))
        lses.append(m + jnp.log(den))
    for b, (r, j) in enumerate(blocks):
        vwin = v_ref[r, pl.ds(windows[b], kw), :]
        o = jnp.zeros((tq, HEAD_BLOCK), F32)
        lse = jnp.zeros((tq, HEAD_BLOCK), F32)
        for h in heads:
            sel = lane_head == h
            o = jnp.where(sel, jnp.dot(probs[b * HEADS_PER_BLOCK + h], vwin, preferred_element_type=F32), o)
            lse = jnp.where(sel, lses[b * HEADS_PER_BLOCK + h], lse)
        o_ref[r, j * tq:(j + 1) * tq, :] = o.astype(BF16)
        lse_ref[r, j * tq:(j + 1) * tq, :] = lse


def _dilated_attention(qkv, group):
    batch, dil, sub_len, _ = qkv.shape
    tq = min(128, sub_len)
    kw = min(256, sub_len)
    rows = min(DILATED_ROWS_PER_STEP, sub_len)
    n_res = min(dil, DILATED_ROWS_PER_STEP // rows)
    qmap = lambda b, r, i: (b, r, i, 0)
    return pl.pallas_call(
        functools.partial(_dilated_body, tq=tq, kw=kw, sub_len=sub_len),
        out_shape=(jax.ShapeDtypeStruct((batch, dil, sub_len, HEAD_BLOCK), BF16),
                   jax.ShapeDtypeStruct((batch, dil, sub_len, HEAD_BLOCK), F32)),
        grid=(batch, dil // n_res, sub_len // rows),
        in_specs=[
            pl.BlockSpec((None, n_res, rows, HEAD_BLOCK), qmap),
            pl.BlockSpec((None, n_res, sub_len, HEAD_BLOCK), lambda b, r, i: (b, r, 0, 1)),
            pl.BlockSpec((None, n_res, sub_len, HEAD_BLOCK), lambda b, r, i: (b, r, 0, 2)),
        ],
        out_specs=(pl.BlockSpec((None, n_res, rows, HEAD_BLOCK), qmap),
                   pl.BlockSpec((None, n_res, rows, HEAD_BLOCK), qmap)),
        compiler_params=_cparams(("arbitrary", "arbitrary", "arbitrary")),
        name=f"dilated_attn_g{group}",
    )(qkv, qkv, qkv)


def _ssm_body(uf_ref, ub_ref, bmat_ref, a_ref, cmat_ref, yf_ref, yb_ref, carry_ref, *state_refs, tm, pitch):
    n_slabs = SSM_N // LANES
    nb = uf_ref.shape[0]
    chains = [(d, b) for d in range(2) for b in range(nb)]

    @pl.when(pl.program_id(1) == 0)
    def _():
        carry_ref[...] = jnp.zeros_like(carry_ref)

    for c, (d, b) in enumerate(chains):
        u = (uf_ref if d == 0 else ub_ref)[b].astype(BF16)
        bu = jnp.dot(u, bmat_ref[d], preferred_element_type=F32)
        for part in range(2):
            for j in range(n_slabs):
                col = part * SSM_N + j * LANES
                state_refs[2 * c + part][j * pitch:j * pitch + tm, :] = bu[:, col:col + LANES]

    coef = [(a_ref[d, 0], a_ref[d, 1]) for d in range(2)]

    def step(t, hs):
        out = []
        for c, (d, b) in enumerate(chains):
            rows = pl.ds(t if d == 0 else tm - 1 - t, n_slabs, stride=pitch)
            ar, ai = coef[d]
            hr, hi = hs[2 * c], hs[2 * c + 1]
            nr = ar * hr - ai * hi + state_refs[2 * c][rows, :]
            ni = ar * hi + ai * hr + state_refs[2 * c + 1][rows, :]
            state_refs[2 * c][rows, :] = nr
            state_refs[2 * c + 1][rows, :] = ni
            out += [nr, ni]
        return tuple(out)

    final = lax.fori_loop(0, tm, step, tuple(carry_ref[k] for k in range(2 * len(chains))), unroll=8)
    for k, v in enumerate(final):
        carry_ref[k] = v

    for c, (d, b) in enumerate(chains):
        y = jnp.zeros((tm, B_WIDTH), F32)
        for part in range(2):
            for j in range(n_slabs):
                h = state_refs[2 * c + part][j * pitch:j * pitch + tm, :].astype(BF16)
                row0 = part * SSM_N + j * LANES
                y = y + jnp.dot(h, cmat_ref[d, row0:row0 + LANES, :], preferred_element_type=F32)
        (yf_ref if d == 0 else yb_ref)[b] = y


def _ssm_scan(u, bmat, coef, cmat, batch, seq):
    tm = SSM_TIME_TILE
    pitch = tm + 8
    nb = 2 if batch % 2 == 0 else 1
    n_t = seq // tm
    n_chains = 2 * nb
    fwd = lambda b, i: (b, i, 0)
    bwd = lambda b, i: (b, n_t - 1 - i, 0)
    const = lambda b, i: (0,) * 3
    u3 = u.reshape(batch, seq, B_WIDTH)
    yf, yb = pl.pallas_call(
        functools.partial(_ssm_body, tm=tm, pitch=pitch),
        out_shape=(jax.ShapeDtypeStruct((batch, seq, B_WIDTH), F32),) * 2,
        grid=(batch // nb, n_t),
        in_specs=[
            pl.BlockSpec((nb, tm, B_WIDTH), fwd),
            pl.BlockSpec((nb, tm, B_WIDTH), bwd),
            pl.BlockSpec(bmat.shape, const),
            pl.BlockSpec(coef.shape, lambda b, i: (0,) * 4),
            pl.BlockSpec(cmat.shape, const),
        ],
        out_specs=(pl.BlockSpec((nb, tm, B_WIDTH), fwd), pl.BlockSpec((nb, tm, B_WIDTH), bwd)),
        scratch_shapes=[pltpu.VMEM((2 * n_chains, 8, LANES), F32)]
        + [pltpu.VMEM((SSM_N // LANES * pitch, LANES), F32) for _ in range(2 * n_chains)],
        compiler_params=_cparams(("arbitrary", "arbitrary")),
        name="ssm_scan",
    )(u3, u3, bmat, coef, cmat)
    return yf.reshape(batch * seq, B_WIDTH), yb.reshape(batch * seq, B_WIDTH)


def _ssm_tables(lam_re, lam_im, log_dt, b_re, b_im, c_re, c_im):
    lr, li = lam_re.astype(F32), lam_im.astype(F32)
    dt = jnp.exp(log_dt.astype(F32))[:, None]
    mag = jnp.exp(lr * dt)
    a_r = mag * jnp.cos(li * dt)
    a_i = mag * jnp.sin(li * dt)
    den = lr * lr + li * li
    f_r = ((a_r - 1.0) * lr + a_i * li) / den
    f_i = (a_i * lr - (a_r - 1.0) * li) / den
    br, bi = b_re.astype(F32), b_im.astype(F32)
    bb_r = f_r[..., None] * br - f_i[..., None] * bi
    bb_i = f_r[..., None] * bi + f_i[..., None] * br
    eye = jnp.eye(B_GROUPS, dtype=F32)

    def in_mat(bb):
        return jnp.einsum('gpc,gh->gchp', bb, eye).reshape(B_WIDTH, SSM_N)

    def out_mat(c):
        return jnp.einsum('gcp,gh->gphc', c.astype(F32), eye).reshape(SSM_N, B_WIDTH)

    bmat = jnp.concatenate([in_mat(bb_r), in_mat(bb_i)], axis=1).astype(BF16)
    cmat = jnp.concatenate([out_mat(c_re), -out_mat(c_im)], axis=0).astype(BF16)

    coef = jnp.stack([a_r.reshape(SSM_N // LANES, LANES), a_i.reshape(SSM_N // LANES, LANES)])
    return bmat, coef, cmat


def _gelu_tanh(x):
    return 0.5 * x * (1.0 + jnp.tanh(math.sqrt(2.0 / math.pi) * (x + 0.044715 * (x * x * x))))


def _token_order(ref, unperm_ref, split):
    tm = ref.shape[0] * ref.shape[1]
    v = ref[...].reshape(tm, ref.shape[2])
    if unperm_ref is None:
        return v.astype(F32)
    if not split:
        return jnp.dot(unperm_ref[...], v, preferred_element_type=F32)
    hi = v.astype(BF16)
    lo = (v - hi.astype(F32)).astype(BF16)
    return (jnp.dot(unperm_ref[...], hi, preferred_element_type=F32)
            + jnp.dot(unperm_ref[...], lo, preferred_element_type=F32))


def _even_out_body(x_ref, o0_ref, o1_ref, o2_ref, l0_ref, l1_ref, l2_ref, u_ref, yf_ref, yb_ref,
                   d_ref, wglu_ref, wout_ref, unperm1_ref, unperm2_ref, out_ref):
    unperms = (None, unperm1_ref, unperm2_ref)
    l0, l1, l2 = (_token_order(r, p, True) for r, p in zip((l0_ref, l1_ref, l2_ref), unperms))
    o0, o1, o2 = (_token_order(r, p, False) for r, p in zip((o0_ref, o1_ref, o2_ref), unperms))
    lmax = jnp.maximum(jnp.maximum(l0, l1), l2)
    e0, e1, e2 = jnp.exp(l0 - lmax), jnp.exp(l1 - lmax), jnp.exp(l2 - lmax)
    den = e0 + e1 + e2
    a = (e0 / den) * o0 + (e1 / den) * o1 + (e2 / den) * o2
    y = u_ref[...] * d_ref[...] + yf_ref[...] + yb_ref[...]
    z = _gelu_tanh(y)
    z = z * jax.nn.sigmoid(jnp.dot(z.astype(BF16), wglu_ref[...], preferred_element_type=F32))
    acc = jnp.dot(a.astype(BF16), wout_ref[0:HEAD_BLOCK, :], preferred_element_type=F32)
    acc = acc + jnp.dot(z.astype(BF16), wout_ref[HEAD_BLOCK:, :], preferred_element_type=F32)
    out_ref[...] = x_ref[...] + acc


def _even_out(x, os_, ls_, u, yf, yb, d_skip, wglu, wout, seq, tm):
    n = x.shape[0]
    n_seq_tiles = seq // tm
    row = lambda i: (i, 0)
    const = lambda i: (0, 0)
    narrow = pl.BlockSpec((tm, HEAD_BLOCK), row)
    grouped = [pl.BlockSpec((None, dil, tm // dil, HEAD_BLOCK), lambda i: (i // n_seq_tiles, 0, i % n_seq_tiles, 0))
               for _, dil in A_GROUPS]
    unperms = [_residue_permutation(tm, dil).T for _, dil in A_GROUPS[1:]]
    return pl.pallas_call(
        _even_out_body,
        out_shape=jax.ShapeDtypeStruct((n, D_MODEL), F32),
        grid=(n // tm,),
        in_specs=[pl.BlockSpec((tm, D_MODEL), row)] + grouped + grouped + [narrow] * 3 + [
            pl.BlockSpec((1, B_WIDTH), const),
            pl.BlockSpec(wglu.shape, const),
            pl.BlockSpec(wout.shape, const),
            pl.BlockSpec((tm, tm), const),
            pl.BlockSpec((tm, tm), const),
        ],
        out_specs=pl.BlockSpec((tm, D_MODEL), row),
        compiler_params=_cparams(("arbitrary",)),
        name="even_out",
    )(x, *os_, *ls_, u, yf, yb, d_skip, wglu, wout, *unperms)


def _ffn_body(x_ref, g_ref, wg_ref, wu_ref, wd_ref, out_ref, *, chunk):
    x = x_ref[...]
    xn = _rms(x, g_ref[...]).astype(BF16)
    n_chunks = D_FF // chunk

    def gate_up(c):
        cols = slice(c * chunk, (c + 1) * chunk)
        return (jnp.dot(xn, wg_ref[:, cols], preferred_element_type=F32),
                jnp.dot(xn, wu_ref[:, cols], preferred_element_type=F32))

    acc = jnp.zeros(x.shape, F32)
    nxt = gate_up(0)
    for c in range(n_chunks):
        g, up = nxt
        if c + 1 < n_chunks:
            nxt = gate_up(c + 1)
        h = (g * jax.nn.sigmoid(g) * up).astype(BF16)
        acc = acc + jnp.dot(h, wd_ref[c * chunk:(c + 1) * chunk, :], preferred_element_type=F32)
    out_ref[...] = x + acc


def _ffn(x, ln, wg, wu, wd, tm):
    n = x.shape[0]
    row = lambda i: (i, 0)
    const = lambda i: (0, 0)
    return pl.pallas_call(
        functools.partial(_ffn_body, chunk=256),
        out_shape=jax.ShapeDtypeStruct((n, D_MODEL), F32),
        grid=(n // tm,),
        in_specs=[
            pl.BlockSpec((tm, D_MODEL), row),
            pl.BlockSpec((1, D_MODEL), const),
            pl.BlockSpec(wg.shape, const),
            pl.BlockSpec(wu.shape, const),
            pl.BlockSpec(wd.shape, const),
        ],
        out_specs=pl.BlockSpec((tm, D_MODEL), row),
        compiler_params=_cparams(("arbitrary",)),
        name="ffn_dense",
    )(x, ln, wg, wu, wd)


def _ple_body(x_ref, p_ref, g_ref, wp_ref, wg_ref, out_ref):
    x = x_ref[...]
    xn = _rms(x, g_ref[...]).astype(BF16)
    gate = jax.nn.sigmoid(jnp.dot(xn, wg_ref[...], preferred_element_type=F32))
    emb = jnp.dot(p_ref[...].astype(BF16), wp_ref[...], preferred_element_type=F32)
    out_ref[...] = x + emb * gate


def _ple(x, p, ln, wp, wg, tm):
    n = x.shape[0]
    row = lambda i: (i, 0)
    const = lambda i: (0, 0)
    return pl.pallas_call(
        _ple_body,
        out_shape=jax.ShapeDtypeStruct((n, D_MODEL), F32),
        grid=(n // tm,),
        in_specs=[
            pl.BlockSpec((tm, D_MODEL), row),
            pl.BlockSpec((tm, PLE_DIM), row),
            pl.BlockSpec((1, D_MODEL), const),
            pl.BlockSpec(wp.shape, const),
            pl.BlockSpec(wg.shape, const),
        ],
        out_specs=pl.BlockSpec((tm, D_MODEL), row),
        compiler_params=_cparams(("arbitrary",)),
        name="per_layer_embed",
    )(x, p, ln, wp, wg)


def _gqa_body(qt_ref, k_ref, vt_ref, o_ref, qs_ref, s_ref, smax_ref, m_ref, acc_ref, *, tq, tk, seq):
    n_kv = seq // tk
    vt_tile = vt_ref.shape[2]
    for c in range(C_Q_WIDTH // HEAD_BLOCK):
        qs_ref[...] = jnp.zeros(qs_ref.shape, BF16)
        for g in range(HEADS_PER_BLOCK):
            rows = slice(c * HEAD_BLOCK + g * HEAD_DIM, c * HEAD_BLOCK + (g + 1) * HEAD_DIM)
            qs_ref[g * HEAD_DIM:(g + 1) * HEAD_DIM, g * tq:(g + 1) * tq] = qt_ref[rows, :]
        m_ref[...] = jnp.full(m_ref.shape, NEG_INF, F32)
        acc_ref[...] = jnp.zeros(acc_ref.shape, F32)

        def scores(t, slot):
            k0 = pl.multiple_of(t * tk, tk)
            s = jnp.dot(k_ref[pl.ds(k0, tk), :], qs_ref[...], preferred_element_type=F32)
            s_ref[slot] = s
            smax_ref[slot] = jnp.max(s, axis=0, keepdims=True)

        def consume(t, slot):
            m_old = m_ref[...]
            m_new = jnp.maximum(m_old, smax_ref[slot])
            alpha = jnp.exp2(m_old - m_new)
            p = jnp.exp2(s_ref[slot] - m_new).astype(BF16)
            m_ref[...] = m_new
            for g in range(HEADS_PER_BLOCK):
                cols = slice(g * tq, (g + 1) * tq)
                pv = alpha[:, cols] * acc_ref[g]
                for part in range(tk // vt_tile):
                    vt = vt_ref[t * (tk // vt_tile) + part, g * V_HEAD_ROWS:(g + 1) * V_HEAD_ROWS, :]
                    pv = pv + jnp.dot(vt, p[part * vt_tile:(part + 1) * vt_tile, cols], preferred_element_type=F32)
                acc_ref[g] = pv

        scores(0, 0)

        def trip(u, last):
            scores(2 * u + 1, 1)
            consume(2 * u, 0)
            if not last:
                scores(2 * u + 2, 0)
            consume(2 * u + 1, 1)

        def step(u, carry):
            trip(u, False)
            return carry

        if n_kv > 2:
            lax.fori_loop(0, n_kv // 2 - 1, step, 0)
        trip(n_kv // 2 - 1, True)
        heads = [acc_ref[g, 0:HEAD_DIM, :] / acc_ref[g, HEAD_DIM:HEAD_DIM + 1, :] for g in range(HEADS_PER_BLOCK)]
        o_ref[:, c * HEAD_BLOCK:(c + 1) * HEAD_BLOCK] = jnp.concatenate(heads, axis=0).T.astype(BF16)


def _gqa_attention(qt, k, vt, batch, seq, tq, tk, tm):
    assert tm % tq == 0 and tk % tm == 0 and seq % (2 * tk) == 0
    per_tm = tm // tq
    tiles_per_seq = seq // tm
    return pl.pallas_call(
        functools.partial(_gqa_body, tq=tq, tk=tk, seq=seq),
        out_shape=jax.ShapeDtypeStruct((batch, seq, C_Q_WIDTH), BF16),
        grid=(batch, seq // tq),
        in_specs=[
            pl.BlockSpec((None, C_Q_WIDTH, tq), lambda b, i: (b * tiles_per_seq + i // per_tm, 0, i % per_tm)),
            pl.BlockSpec((None, seq, C_KV_WIDTH), lambda b, i: (b, 0, 0)),
            pl.BlockSpec((tiles_per_seq, HEADS_PER_BLOCK * V_HEAD_ROWS, tm), lambda b, i: (b, 0, 0)),
        ],
        out_specs=pl.BlockSpec((None, tq, C_Q_WIDTH), lambda b, i: (b, i, 0)),
        scratch_shapes=[pltpu.VMEM((HEAD_BLOCK, HEADS_PER_BLOCK * tq), BF16),
                        pltpu.VMEM((2, tk, HEADS_PER_BLOCK * tq), F32),
                        pltpu.VMEM((2, 1, HEADS_PER_BLOCK * tq), F32),
                        pltpu.VMEM((1, HEADS_PER_BLOCK * tq), F32),
                        pltpu.VMEM((HEADS_PER_BLOCK, V_HEAD_ROWS, tq), F32)],
        compiler_params=_cparams(("arbitrary", "arbitrary")),
        name="gqa_attention",
    )(qt, k.reshape(batch, seq, C_KV_WIDTH), vt).reshape(batch * seq, C_Q_WIDTH)


def _odd_out_body(x_ref, a_ref, wout_ref, g_ref, rw_hi_ref, rw_lo_ref, out_ref, xn_ref, route_ref, route_t_ref):
    x = x_ref[...] + jnp.dot(a_ref[...], wout_ref[...], preferred_element_type=F32)
    out_ref[...] = x
    h = _rms(x, g_ref[...])
    h_hi = h.astype(BF16)
    h_lo = (h - h_hi.astype(F32)).astype(BF16)
    logits = (jnp.dot(h_hi, rw_hi_ref[...], preferred_element_type=F32)
              + jnp.dot(h_hi, rw_lo_ref[...], preferred_element_type=F32)
              + jnp.dot(h_lo, rw_hi_ref[...], preferred_element_type=F32))
    lane = lax.broadcasted_iota(jnp.int32, logits.shape, 1)
    logits = jnp.where(lane < N_EXPERTS, logits, -jnp.inf)
    m1 = jnp.max(logits, axis=-1, keepdims=True)
    i1 = jnp.min(jnp.where(logits == m1, lane, LANES), axis=-1, keepdims=True)
    rest = jnp.where(lane == i1, -jnp.inf, logits)
    m2 = jnp.max(rest, axis=-1, keepdims=True)
    i2 = jnp.min(jnp.where(rest == m2, lane, LANES), axis=-1, keepdims=True)
    e2 = jnp.exp(m2 - m1)
    w1 = 1.0 / (1.0 + e2)
    w2 = e2 / (1.0 + e2)
    rec = jnp.where(lane == 0, i1.astype(F32),
                    jnp.where(lane == 1, i2.astype(F32), jnp.where(lane == 2, w1, jnp.where(lane == 3, w2, 0.0))))
    route_ref[...] = rec
    route_t_ref[...] = rec.T[0:8, :]
    xn_ref[...] = _pack_pairs(h_hi[:, :D_MODEL // 2].astype(F32), h_hi[:, D_MODEL // 2:].astype(F32))


def _odd_out(x, attn, wout, ln, rw_hi, rw_lo, tm):
    n = x.shape[0]
    row = lambda i: (i, 0)
    const = lambda i: (0, 0)
    return pl.pallas_call(
        _odd_out_body,
        out_shape=(jax.ShapeDtypeStruct((n, D_MODEL), F32), jax.ShapeDtypeStruct((n, D_MODEL // 2), jnp.uint32),
                   jax.ShapeDtypeStruct((n, LANES), F32), jax.ShapeDtypeStruct((8, n), F32)),
        grid=(n // tm,),
        in_specs=[
            pl.BlockSpec((tm, D_MODEL), row),
            pl.BlockSpec((tm, C_Q_WIDTH), row),
            pl.BlockSpec(wout.shape, const),
            pl.BlockSpec((1, D_MODEL), const),
            pl.BlockSpec(rw_hi.shape, const),
            pl.BlockSpec(rw_lo.shape, const),
        ],
        out_specs=(pl.BlockSpec((tm, D_MODEL), row), pl.BlockSpec((tm, D_MODEL // 2), row),
                   pl.BlockSpec((tm, LANES), row), pl.BlockSpec((8, tm), lambda i: (0, i))),
        compiler_params=_cparams(("arbitrary",)),
        name="odd_out_router",
    )(x, attn, wout, ln, rw_hi, rw_lo)


def _pack_pairs(lo, hi):
    lo_bits = lax.bitcast_convert_type(lo.astype(BF16).astype(F32), jnp.uint32) >> 16
    hi_bits = lax.bitcast_convert_type(hi.astype(BF16).astype(F32), jnp.uint32) & jnp.uint32(0xFFFF0000)
    return lo_bits | hi_bits


def _unpack_pairs(u):
    lo = lax.bitcast_convert_type(u << 16, F32)
    hi = lax.bitcast_convert_type(u & jnp.uint32(0xFFFF0000), F32)
    return lo, hi


def _rank_body(route_ref, tri_ref, pos_ref, meta_ref, cnt_ref, off_ref, *, tile_rows):
    phase = pl.program_id(0)
    i = pl.program_id(1)
    tm = route_ref.shape[1]
    route = route_ref[...]
    i1 = route[0:1, :]
    i2 = route[1:2, :]
    row = lax.broadcasted_iota(jnp.int32, (N_EXPERTS, tm), 0).astype(F32)
    hit = ((row == i1) | (row == i2)).astype(F32)
    per_expert = jnp.sum(hit, axis=1, keepdims=True)
    erow = lax.broadcasted_iota(jnp.int32, (N_EXPERTS, 1), 0)

    @pl.when((phase == 0) & (i == 0))
    def _():
        cnt_ref[...] = jnp.zeros_like(cnt_ref)

    @pl.when(phase == 0)
    def _():
        cnt_ref[...] += per_expert

    @pl.when((phase == 1) & (i == 0))
    def _():
        tot = cnt_ref[...]
        padded = jnp.floor((tot + (tile_rows - 1)) * (1.0 / tile_rows)) * tile_rows
        off = jnp.zeros_like(tot)
        for e in range(N_EXPERTS - 1):
            pe = jnp.sum(jnp.where(erow == e, padded, 0.0), axis=0, keepdims=True)
            off = off + jnp.where(erow > e, pe, 0.0)
        off_ref[...] = off
        lane = lax.broadcasted_iota(jnp.int32, meta_ref.shape, 1)
        meta_ref[...] = jnp.where(lane == 0, off, jnp.where(lane == 1, tot, 0.0))
        cnt_ref[...] = jnp.zeros_like(cnt_ref)

    @pl.when(phase == 1)
    def _():
        hit16 = jnp.concatenate([hit, jnp.zeros_like(hit)], axis=0).astype(BF16)
        before = jnp.dot(hit16, tri_ref[...], preferred_element_type=F32)[0:N_EXPERTS, :]
        slot_row = off_ref[...] + cnt_ref[...] + before
        pos1 = jnp.sum(jnp.where(row == i1, slot_row, 0.0), axis=0, keepdims=True)
        pos2 = jnp.sum(jnp.where(row == i2, slot_row, 0.0), axis=0, keepdims=True)
        pos_ref[...] = jnp.concatenate([pos1, pos2, jnp.zeros((6, tm), F32)], axis=0).astype(jnp.int32)
        cnt_ref[...] += per_expert


def _rank(route_t, tri, tm, tile_rows):
    n = route_t.shape[1]
    return pl.pallas_call(
        functools.partial(_rank_body, tile_rows=tile_rows),
        out_shape=(jax.ShapeDtypeStruct((n // tm, 8, tm), jnp.int32), jax.ShapeDtypeStruct((N_EXPERTS, LANES), F32)),
        grid=(2, n // tm),
        in_specs=[pl.BlockSpec((8, tm), lambda ph, i: (0, i)), pl.BlockSpec(tri.shape, lambda ph, i: (0, 0))],
        out_specs=(pl.BlockSpec((None, 8, tm), lambda ph, i: (i * ph, 0, 0)),
                   pl.BlockSpec((N_EXPERTS, LANES), lambda ph, i: (0, 0))),
        scratch_shapes=[pltpu.VMEM((N_EXPERTS, 1), F32), pltpu.VMEM((N_EXPERTS, 1), F32)],
        compiler_params=_cparams(("arbitrary", "arbitrary")),
        name="moe_rank",
    )(route_t, tri)


def _row_copy(src, src_row, dst, dst_row, sem):
    return pltpu.make_async_copy(src.at[pl.ds(src_row, 1)], dst.at[pl.ds(dst_row, 1)], sem)


def _dispatch_body(pos_ref, xn_ref, buf_ref, xs_ref, sem, *, tm):
    del buf_ref

    def issue(t, carry):
        _row_copy(xn_ref, t, xs_ref, pos_ref[0, t], sem).start()
        _row_copy(xn_ref, t, xs_ref, pos_ref[1, t], sem).start()
        return carry

    lax.fori_loop(0, tm, issue, 0, unroll=8)
    for _ in range(TOP_K):
        pltpu.make_async_copy(xn_ref, xs_ref.at[pl.ds(0, tm)], sem).wait()


def _dispatch(pos, xn, rows, tm):
    n = xn.shape[0]
    buf = jnp.zeros((rows, xn.shape[1]), xn.dtype)
    return pl.pallas_call(
        functools.partial(_dispatch_body, tm=tm),
        out_shape=jax.ShapeDtypeStruct(buf.shape, buf.dtype),
        grid=(n // tm,),
        in_specs=[pl.BlockSpec((None, 8, tm), lambda i: (i, 0, 0), memory_space=pltpu.SMEM),
                  pl.BlockSpec((tm, xn.shape[1]), lambda i: (i, 0)), pl.BlockSpec(memory_space=pl.ANY)],
        out_specs=pl.BlockSpec(memory_space=pl.ANY),
        scratch_shapes=[pltpu.SemaphoreType.DMA(())],
        input_output_aliases={2: 0},
        compiler_params=_cparams(("arbitrary",)),
        name="moe_dispatch",
    )(pos, xn, buf)


def _expert_ffn_body(te_ref, tv_ref, xs_ref, wg_ref, wu_ref, wd_ref, ys_ref, xa_ref, xb_ref, acc_ref):
    del te_ref
    j = pl.program_id(0)
    c = pl.program_id(1)
    half = D_MODEL // 2

    @pl.when(c == 0)
    def _():
        lo, hi = _unpack_pairs(xs_ref[...])
        xa_ref[...] = lo.astype(BF16)
        xb_ref[...] = hi.astype(BF16)
        acc_ref[...] = jnp.zeros_like(acc_ref)

    @pl.when(tv_ref[j] != 0)
    def _():
        n_split = 2
        rows = xa_ref.shape[0] // n_split
        hidden = []
        for part in range(n_split):
            rs = slice(part * rows, (part + 1) * rows)
            xa, xb = xa_ref[rs, :], xb_ref[rs, :]
            g = (jnp.dot(xa, wg_ref[0:half, :], preferred_element_type=F32)
                 + jnp.dot(xb, wg_ref[half:, :], preferred_element_type=F32))
            up = (jnp.dot(xa, wu_ref[0:half, :], preferred_element_type=F32)
                  + jnp.dot(xb, wu_ref[half:, :], preferred_element_type=F32))
            hidden.append((g, up))
        for part, (g, up) in enumerate(hidden):
            rs = slice(part * rows, (part + 1) * rows)
            h = (g * jax.nn.sigmoid(g) * up).astype(BF16)
            acc_ref[rs, :] += jnp.dot(h, wd_ref[...], preferred_element_type=F32)

    @pl.when(c == pl.num_programs(1) - 1)
    def _():
        acc = acc_ref[...]
        ys_ref[...] = _pack_pairs(acc[:, :half], acc[:, half:])


def _expert_ffn(tile_expert, tile_valid, xs, wg, wu, wd, tile_rows, chunk):
    rows, half = xs.shape
    n_chunks = D_EXPERT // chunk

    def chunk_of(j, c, tv):
        return jnp.where(tv[j] != 0, c, n_chunks - 1)

    grid_spec = pltpu.PrefetchScalarGridSpec(
        num_scalar_prefetch=2,
        grid=(rows // tile_rows, n_chunks),
        in_specs=[
            pl.BlockSpec((tile_rows, half), lambda j, c, te, tv: (j, 0)),
            pl.BlockSpec((None, D_MODEL, chunk), lambda j, c, te, tv: (te[j], 0, chunk_of(j, c, tv))),
            pl.BlockSpec((None, D_MODEL, chunk), lambda j, c, te, tv: (te[j], 0, chunk_of(j, c, tv))),
            pl.BlockSpec((None, chunk, D_MODEL), lambda j, c, te, tv: (te[j], chunk_of(j, c, tv), 0)),
        ],
        out_specs=pl.BlockSpec((tile_rows, half), lambda j, c, te, tv: (j, 0)),
        scratch_shapes=[pltpu.VMEM((tile_rows, half), BF16), pltpu.VMEM((tile_rows, half), BF16),
                        pltpu.VMEM((tile_rows, D_MODEL), F32)],
    )
    return pl.pallas_call(
        _expert_ffn_body,
        out_shape=jax.ShapeDtypeStruct(xs.shape, xs.dtype),
        grid_spec=grid_spec,
        compiler_params=_cparams(("arbitrary", "arbitrary")),
        name="moe_expert_ffn",
    )(tile_expert, tile_valid, xs, wg, wu, wd)


def _moe_routed(xn, route_t, wg, wu, wd):
    n = xn.shape[0]
    tm = MOE_TOKEN_TILE
    tile_rows = MOE_ROW_TILE
    rows = TOP_K * n + N_EXPERTS * tile_rows
    tri = jnp.asarray(np.triu(np.ones((tm, tm), np.float32), k=1), dtype=BF16)
    pos, meta = _rank(route_t, tri, tm, tile_rows)
    seg_end = meta[:, 0] + jnp.ceil(meta[:, 1] / tile_rows) * tile_rows
    tile_start = (jnp.arange(rows // tile_rows) * tile_rows).astype(F32)
    tile_expert = jnp.minimum(jnp.sum(tile_start[:, None] >= seg_end[None, :], axis=1), N_EXPERTS - 1).astype(jnp.int32)
    tile_valid = (tile_start < seg_end[N_EXPERTS - 1]).astype(jnp.int32)
    xs = _dispatch(pos, xn, rows, tm)
    return pos, _expert_ffn(tile_expert, tile_valid, xs, wg, wu, wd, tile_rows, MOE_FF_CHUNK)


def _moe_ple_body(pos_ref, ys_ref, x_ref, route_ref, p_ref, g_ref, wp_ref, wg_ref, out_ref, y1_ref, y2_ref, sem):
    tm = x_ref.shape[0]

    def issue(t, carry):
        _row_copy(ys_ref, pos_ref[0, t], y1_ref, t, sem).start()
        _row_copy(ys_ref, pos_ref[1, t], y2_ref, t, sem).start()
        return carry

    lax.fori_loop(0, tm, issue, 0, unroll=8)
    pltpu.make_async_copy(ys_ref.at[pl.ds(0, tm)], y1_ref, sem).wait()
    pltpu.make_async_copy(ys_ref.at[pl.ds(0, tm)], y2_ref, sem).wait()
    route = route_ref[...]
    w1 = route[:, 2:3]
    w2 = route[:, 3:4]
    lo1, hi1 = _unpack_pairs(y1_ref[...])
    lo2, hi2 = _unpack_pairs(y2_ref[...])
    moe = jnp.concatenate([w1 * lo1 + w2 * lo2, w1 * hi1 + w2 * hi2], axis=1)
    x = x_ref[...] + moe
    xn = _rms(x, g_ref[...]).astype(BF16)
    gate = jax.nn.sigmoid(jnp.dot(xn, wg_ref[...], preferred_element_type=F32))
    emb = jnp.dot(p_ref[...].astype(BF16), wp_ref[...], preferred_element_type=F32)
    out_ref[...] = x + emb * gate


def _moe_ple(pos, ys, x, route, p, ln, wp, wg):
    n = x.shape[0]
    tm = pos.shape[2]
    row = lambda i: (i, 0)
    const = lambda i: (0, 0)
    return pl.pallas_call(
        _moe_ple_body,
        out_shape=jax.ShapeDtypeStruct((n, D_MODEL), F32),
        grid=(n // tm,),
        in_specs=[
            pl.BlockSpec((None, 8, tm), lambda i: (i, 0, 0), memory_space=pltpu.SMEM),
            pl.BlockSpec(memory_space=pl.ANY),
            pl.BlockSpec((tm, D_MODEL), row),
            pl.BlockSpec((tm, LANES), row),
            pl.BlockSpec((tm, PLE_DIM), row),
            pl.BlockSpec((1, D_MODEL), const),
            pl.BlockSpec(wp.shape, const),
            pl.BlockSpec(wg.shape, const),
        ],
        out_specs=pl.BlockSpec((tm, D_MODEL), row),
        scratch_shapes=[pltpu.VMEM((tm, D_MODEL // 2), jnp.uint32), pltpu.VMEM((tm, D_MODEL // 2), jnp.uint32),
                        pltpu.SemaphoreType.DMA(())],
        compiler_params=_cparams(("arbitrary",)),
        name="moe_combine_embed",
    )(pos, ys, x, route, p, ln, wp, wg)


def _rope_tables(pos, dim):
    inv = ROPE_THETA ** (-jnp.arange(0, dim, 2, dtype=F32) / dim)
    ang = pos.astype(F32)[:, None] * inv[None, :]
    ang = jnp.concatenate([ang, ang], axis=-1)
    return jnp.cos(ang), jnp.sin(ang)


def _tile_heads(t):
    return jnp.tile(t, (1, HEADS_PER_BLOCK))


def _even_rope(seq):
    cos, sin = _rope_tables(jnp.arange(seq), HEAD_DIM)
    sign = jnp.where(jnp.arange(HEAD_DIM) < HEAD_DIM // 2, -1.0, 1.0).astype(F32)
    return _tile_heads(cos), _tile_heads(sin * sign)


def _odd_rope(seq):
    t = jnp.arange(seq)
    half = HEAD_DIM // 2
    cr, sr = _rope_tables(t // GRID_W, half)
    cc, sc = _rope_tables(t % GRID_W, half)
    sign = jnp.where(jnp.arange(half) < half // 2, -1.0, 1.0).astype(F32)
    cos = jnp.concatenate([cr, cc], axis=-1)
    sin = jnp.concatenate([sr * sign, sc * sign], axis=-1)
    return _tile_heads(cos), _tile_heads(sin)


def _head_mean_matrix():
    blk = np.arange(HEAD_BLOCK) // HEAD_DIM
    return jnp.asarray((blk[:, None] == blk[None, :]).astype(np.float32) / HEAD_DIM, dtype=BF16)


def _q_head_permutation():
    n_blocks = C_Q_WIDTH // HEAD_BLOCK
    heads = [HEADS_PER_BLOCK * g + c for c in range(n_blocks) for g in range(HEADS_PER_BLOCK)]
    return np.concatenate([np.arange(h * HEAD_DIM, (h + 1) * HEAD_DIM) for h in heads])


def _row(v):
    return v.astype(F32).reshape(1, -1)


def _even_layer(x, batch, seq, w):
    tm = 512
    cos, sin = _even_rope(seq)
    *qkvs, u = _proj_even(x, w['ln_mix'], w['w_in'], w['bd'], w['qn'], w['kn'], cos, sin, batch, seq, tm)
    outs, lses = [], []
    for g, qkv in enumerate(qkvs):
        o, lse = _dilated_attention(qkv, g)
        outs.append(o)
        lses.append(lse)
    tabs = [_ssm_tables(*(w[k][direction] for k in ('lam_re', 'lam_im', 'log_dt', 'b_re', 'b_im', 'c_re', 'c_im')))
            for direction in range(2)]
    ys = _ssm_scan(u, *(jnp.stack(t) for t in zip(*tabs)), batch, seq)
    x = _even_out(x, outs, lses, u, ys[0], ys[1], w['d_skip'], w['w_glu'], w['w_out'], seq, tm)
    return _ffn(x, w['ln_ffn'], w['ffn_gate'], w['ffn_up'], w['ffn_down'], 512)


def _odd_layer(x, p, batch, seq, w, ple):
    tm = 512
    cos, sin = _odd_rope(seq)
    qt, k, vt = _proj_odd(x, w['ln_mix'], w['w_in'], w['bd'], w['qn'], w['kn'], cos, sin, seq, tm)
    attn = _gqa_attention(qt, k, vt, batch, seq, 256, GQA_KV_TILE, tm)
    x, xn, route, route_t = _odd_out(x, attn, w['w_out'], w['ln_ffn'], w['router_hi'], w['router_lo'],
                                     MOE_TOKEN_TILE)
    pos, ys = _moe_routed(xn, route_t, w['moe_gate'], w['moe_up'], w['moe_down'])
    return _moe_ple(pos, ys, x, route, p, *ple)


def _prepare_even(j, ln_mix_e, w_in_e, a_qnorm, a_knorm, lam_re, lam_im, log_dt, b_re, b_im, c_re, c_im,
                  ssm_d, ssm_w_glu, w_out_e, ln_ffn_e, ffn_w_gate, ffn_w_up, ffn_w_down):
    return dict(
        ln_mix=_row(ln_mix_e[j]), w_in=w_in_e[j].astype(BF16), bd=_head_mean_matrix(),
        qn=_tile_heads(_row(a_qnorm[j])), kn=_tile_heads(_row(a_knorm[j])),
        lam_re=lam_re[j], lam_im=lam_im[j], log_dt=log_dt[j], b_re=b_re[j], b_im=b_im[j],
        c_re=c_re[j], c_im=c_im[j], d_skip=_row(ssm_d[j]), w_glu=ssm_w_glu[j].astype(BF16),
        w_out=w_out_e[j].astype(BF16), ln_ffn=_row(ln_ffn_e[j]), ffn_gate=ffn_w_gate[j].astype(BF16),
        ffn_up=ffn_w_up[j].astype(BF16), ffn_down=ffn_w_down[j].astype(BF16))


def _prepare_odd(j, ln_mix_o, w_in_o, c_qnorm, c_knorm, w_out_o, ln_ffn_o, router_w, moe_w_gate, moe_w_up,
                 moe_w_down):
    perm = _q_head_permutation()
    w_in = w_in_o[j]
    w_in = jnp.concatenate([w_in[:, :C_Q_WIDTH][:, perm], w_in[:, C_Q_WIDTH:]], axis=1).astype(BF16)
    rw = jnp.pad(router_w[j].astype(F32), ((0, 0), (0, LANES - N_EXPERTS)))
    rw_hi = rw.astype(BF16)
    rw_lo = (rw - rw_hi.astype(F32)).astype(BF16)
    return dict(
        ln_mix=_row(ln_mix_o[j]), w_in=w_in, bd=_head_mean_matrix(),
        qn=_tile_heads(_row(c_qnorm[j])), kn=_tile_heads(_row(c_knorm[j])),
        w_out=w_out_o[j][perm, :].astype(BF16), ln_ffn=_row(ln_ffn_o[j]), router_hi=rw_hi, router_lo=rw_lo,
        moe_gate=moe_w_gate[j].astype(BF16), moe_up=moe_w_up[j].astype(BF16), moe_down=moe_w_down[j].astype(BF16))


def _trunk(x, p, layers, ple):
    batch, seq, _ = x.shape
    x = x.reshape(batch * seq, D_MODEL)
    for i, w in enumerate(layers):
        pi = p[i].reshape(batch * seq, PLE_DIM)
        if i % 2 == 0:
            x = _ple(_even_layer(x, batch, seq, w), pi, *ple[i], 512)
        else:
            x = _odd_layer(x, pi, batch, seq, w, ple[i])
    return x.reshape(batch, seq, D_MODEL)


def kernel(x_prompt, x_sample, p_prompt, p_sample, ln_mix_e, w_in_e, a_qnorm, a_knorm, ssm_lam_re, ssm_lam_im, ssm_log_dt, ssm_b_re, ssm_b_im, ssm_c_re, ssm_c_im, ssm_d, ssm_w_glu, w_out_e, ln_ffn_e, ffn_w_gate, ffn_w_up, ffn_w_down, ln_mix_o, w_in_o, c_qnorm, c_knorm, w_out_o, ln_ffn_o, router_w, moe_w_gate, moe_w_up, moe_w_down, ple_ln, ple_w_proj, ple_w_gate):
    depth = p_prompt.shape[0]
    layers = []
    for i in range(depth):
        j = i // 2
        if i % 2 == 0:
            layers.append(_prepare_even(j, ln_mix_e, w_in_e, a_qnorm, a_knorm, ssm_lam_re, ssm_lam_im, ssm_log_dt,
                                        ssm_b_re, ssm_b_im, ssm_c_re, ssm_c_im, ssm_d, ssm_w_glu, w_out_e, ln_ffn_e,
                                        ffn_w_gate, ffn_w_up, ffn_w_down))
        else:
            layers.append(_prepare_odd(j, ln_mix_o, w_in_o, c_qnorm, c_knorm, w_out_o, ln_ffn_o, router_w,
                                       moe_w_gate, moe_w_up, moe_w_down))
    ple = [(_row(ple_ln[i]), ple_w_proj[i].astype(BF16), ple_w_gate[i].astype(BF16)) for i in range(depth)]
    y_prompt = _trunk(x_prompt, p_prompt, layers, ple)
    y_sample = _trunk(x_sample, p_sample, layers, ple)
    return (y_prompt, y_sample)
```

```python
import functools
import math

import jax
import jax.numpy as jnp
import numpy as np
from jax import lax
from jax.experimental import pallas as pl
from jax.experimental.pallas import tpu as pltpu

F32 = jnp.float32
BF16 = jnp.bfloat16

D_MODEL = 1024
HEAD_DIM = 64
EPS = 1e-6
ROPE_THETA = 10000.0
NEG_INF = -1e30
LOG2_E = math.log2(math.e)
LANES = 128
HEAD_BLOCK = 256
HEADS_PER_BLOCK = HEAD_BLOCK // HEAD_DIM
A_GROUPS = ((128, 1), (512, 4), (2048, 16))
A_HALF = 64
DILATED_ROWS_PER_STEP = 512
A_WIDTH = 768
B_WIDTH = 256
B_GROUP_CH = 16
B_GROUPS = 16
B_STATE = 64
SSM_N = B_GROUPS * B_STATE
SSM_TIME_TILE = 256
QKV_E = 3 * A_WIDTH
D_FF = 2816
GRID_W = 64
C_Q_WIDTH = 1024
C_KV_WIDTH = 256
V_ONES_ROWS = 16
V_HEAD_ROWS = HEAD_DIM + V_ONES_ROWS
GQA_KV_TILE = 1024
N_EXPERTS = 8
TOP_K = 2
D_EXPERT = 3584
MOE_TOKEN_TILE = 512
MOE_ROW_TILE = 512
MOE_FF_CHUNK = 512
PLE_DIM = 256

VMEM_LIMIT = 56 * 1024 * 1024


def _cparams(sem):
    return pltpu.CompilerParams(dimension_semantics=sem, vmem_limit_bytes=VMEM_LIMIT)


def _rms(x, g):
    ms = jnp.mean(x * x, axis=-1, keepdims=True)
    return x * lax.rsqrt(ms + EPS) * g


def _rope(yn, cos, sin_signed, half):
    width = yn.shape[1]
    lane = lax.broadcasted_iota(jnp.int32, yn.shape, 1)
    up = pltpu.roll(yn, width - half, 1)
    dn = pltpu.roll(yn, half, 1)
    rot = jnp.where((lane & (2 * half - 1)) < half, up, dn)
    return yn * cos + rot * sin_signed


def _proj_even_body(x_ref, g_ref, w_ref, bd_ref, qn_ref, kn_ref, cos_ref, sin_ref, perm1_ref, perm2_ref,
                    qkv0_ref, qkv1_ref, qkv2_ref, u_ref):
    xn = _rms(x_ref[...], g_ref[...]).astype(BF16)
    n_groups = A_WIDTH // HEAD_BLOCK
    outs = (qkv0_ref, qkv1_ref, qkv2_ref)
    perms = (None, perm1_ref, perm2_ref)
    n_blocks = QKV_E // HEAD_BLOCK
    ys = [jnp.dot(xn, w_ref[:, c * HEAD_BLOCK:(c + 1) * HEAD_BLOCK], preferred_element_type=F32)
          for c in range(n_blocks + 1)]
    u_ref[...] = ys[n_blocks]
    sums = [jnp.dot((ys[c] * ys[c]).astype(BF16), bd_ref[...], preferred_element_type=F32)
            for c in range(2 * n_groups)]
    for c in range(2 * n_groups):
        scale = (HEAD_DIM ** -0.5) if c < n_groups else 1.0
        gain = qn_ref[...] if c < n_groups else kn_ref[...]
        ys[c] = _rope(ys[c] * lax.rsqrt(sums[c] + EPS) * gain, cos_ref[...], sin_ref[...], HEAD_DIM // 2) * scale
    ys = [y.astype(BF16) for y in ys[:n_blocks]]
    for c in range(n_blocks):
        group = c % n_groups
        if perms[group] is not None:
            ys[c] = jnp.dot(perms[group][...], ys[c], preferred_element_type=F32).astype(BF16)
    for c in range(n_blocks):
        part, group = divmod(c, n_groups)
        out = outs[group]
        dil, rows = out.shape[0], out.shape[1]
        for r in range(dil):
            out[r, :, part * HEAD_BLOCK:(part + 1) * HEAD_BLOCK] = ys[c][r * rows:(r + 1) * rows, :]


def _residue_permutation(tm, dil):
    dst = np.arange(tm)
    src = (dst % (tm // dil)) * dil + dst // (tm // dil)
    return jnp.asarray((src[:, None] == np.arange(tm)[None, :]).astype(np.float32), dtype=BF16)


def _proj_even(x, ln, w, bd, qn, kn, cos, sin, batch, seq, tm):
    n = x.shape[0]
    n_seq_tiles = seq // tm
    const = lambda i: (0, 0)
    perms = [_residue_permutation(tm, dil) for _, dil in A_GROUPS[1:]]
    qkv_shapes = [jax.ShapeDtypeStruct((batch, dil, seq // dil, 3 * HEAD_BLOCK), BF16) for _, dil in A_GROUPS]
    qkv_specs = [pl.BlockSpec((None, dil, tm // dil, 3 * HEAD_BLOCK),
                              lambda i: (i // n_seq_tiles, 0, i % n_seq_tiles, 0)) for _, dil in A_GROUPS]
    return pl.pallas_call(
        _proj_even_body,
        out_shape=(*qkv_shapes, jax.ShapeDtypeStruct((n, B_WIDTH), F32)),
        grid=(n // tm,),
        in_specs=[
            pl.BlockSpec((tm, D_MODEL), lambda i: (i, 0)),
            pl.BlockSpec((1, D_MODEL), const),
            pl.BlockSpec(w.shape, const),
            pl.BlockSpec(bd.shape, const),
            pl.BlockSpec((1, HEAD_BLOCK), const),
            pl.BlockSpec((1, HEAD_BLOCK), const),
            pl.BlockSpec((tm, HEAD_BLOCK), lambda i: (i % n_seq_tiles, 0)),
            pl.BlockSpec((tm, HEAD_BLOCK), lambda i: (i % n_seq_tiles, 0)),
            pl.BlockSpec((tm, tm), const),
            pl.BlockSpec((tm, tm), const),
        ],
        out_specs=(*qkv_specs, pl.BlockSpec((tm, B_WIDTH), lambda i: (i, 0))),
        compiler_params=_cparams(("arbitrary",)),
        name="proj_even",
    )(x, ln, w, bd, qn, kn, cos, sin, *perms)


def _proj_odd_body(x_ref, g_ref, w_ref, bd_ref, qn_ref, kn_ref, cos_ref, sin_ref, qt_ref, k_ref, vt_ref):
    xn = _rms(x_ref[...], g_ref[...]).astype(BF16)
    tm = xn.shape[0]
    n_q_blocks = C_Q_WIDTH // HEAD_BLOCK
    ys = [jnp.dot(xn, w_ref[:, c * HEAD_BLOCK:(c + 1) * HEAD_BLOCK], preferred_element_type=F32)
          for c in range(n_q_blocks + 2)]
    sums = [jnp.dot((ys[c] * ys[c]).astype(BF16), bd_ref[...], preferred_element_type=F32)
            for c in range(n_q_blocks + 1)]
    for c in range(n_q_blocks + 2):
        y = ys[c]
        if c <= n_q_blocks:
            gain = qn_ref[...] if c < n_q_blocks else kn_ref[...]
            y = _rope(y * lax.rsqrt(sums[c] + EPS) * gain, cos_ref[...], sin_ref[...], HEAD_DIM // 4)
        if c < n_q_blocks:
            qt_ref[c * HEAD_BLOCK:(c + 1) * HEAD_BLOCK, :] = (y * (HEAD_DIM ** -0.5 * LOG2_E)).T.astype(BF16)
        elif c == n_q_blocks:
            k_ref[...] = y.astype(BF16)
        else:
            yt = y.T.astype(BF16)
            ones = jnp.ones((V_ONES_ROWS, tm), BF16)
            for g in range(HEADS_PER_BLOCK):
                vt_ref[g * V_HEAD_ROWS:g * V_HEAD_ROWS + HEAD_DIM, :] = yt[g * HEAD_DIM:(g + 1) * HEAD_DIM, :]
                vt_ref[g * V_HEAD_ROWS + HEAD_DIM:(g + 1) * V_HEAD_ROWS, :] = ones


def _proj_odd(x, ln, w, bd, qn, kn, cos, sin, seq, tm):
    n = x.shape[0]
    n_seq_tiles = seq // tm
    const = lambda i: (0, 0)
    return pl.pallas_call(
        _proj_odd_body,
        out_shape=(jax.ShapeDtypeStruct((n // tm, C_Q_WIDTH, tm), BF16),
                   jax.ShapeDtypeStruct((n, C_KV_WIDTH), BF16),
                   jax.ShapeDtypeStruct((n // tm, HEADS_PER_BLOCK * V_HEAD_ROWS, tm), BF16)),
        grid=(n // tm,),
        in_specs=[
            pl.BlockSpec((tm, D_MODEL), lambda i: (i, 0)),
            pl.BlockSpec((1, D_MODEL), const),
            pl.BlockSpec(w.shape, const),
            pl.BlockSpec(bd.shape, const),
            pl.BlockSpec((1, HEAD_BLOCK), const),
            pl.BlockSpec((1, HEAD_BLOCK), const),
            pl.BlockSpec((tm, HEAD_BLOCK), lambda i: (i % n_seq_tiles, 0)),
            pl.BlockSpec((tm, HEAD_BLOCK), lambda i: (i % n_seq_tiles, 0)),
        ],
        out_specs=(pl.BlockSpec((None, C_Q_WIDTH, tm), lambda i: (i, 0, 0)),
                   pl.BlockSpec((tm, C_KV_WIDTH), lambda i: (i, 0)),
                   pl.BlockSpec((None, HEADS_PER_BLOCK * V_HEAD_ROWS, tm), lambda i: (i, 0, 0))),
        compiler_params=_cparams(("arbitrary",)),
        name="proj_odd",
    )(x, ln, w, bd, qn, kn, cos, sin)


def _dilated_body(q_ref, k_ref, v_ref, o_ref, lse_ref, *, tq, kw, sub_len):
    n_res, rows = q_ref.shape[0], q_ref.shape[1]
    base = pl.program_id(2) * rows
    lane_head = lax.broadcasted_iota(jnp.int32, (tq, HEAD_BLOCK), 1) // HEAD_DIM
    blocks = [(r, j) for r in range(n_res) for j in range(rows // tq)]
    heads = range(HEADS_PER_BLOCK)
    scores, windows = [], []
    for r, j in blocks:
        q0 = base + j * tq
        k0 = pl.multiple_of(jnp.clip(q0 - A_HALF, 0, sub_len - kw), A_HALF)
        q = q_ref[r, j * tq:(j + 1) * tq, :]
        kwin = k_ref[r, pl.ds(k0, kw), :]
        qpos = q0 + lax.broadcasted_iota(jnp.int32, (tq, kw), 0)
        kpos = k0 + lax.broadcasted_iota(jnp.int32, (tq, kw), 1)
        valid = jnp.abs(kpos - qpos) <= A_HALF
        windows.append(k0)
        for h in heads:
            qh = jnp.where(lane_head == h, q, jnp.zeros_like(q))
            s = lax.dot_general(qh, kwin, (((1,), (1,)), ((), ())), preferred_element_type=F32)
            scores.append(jnp.where(valid, s, NEG_INF))
    probs, lses = [], []
    for s in scores:
        m = jnp.max(s, axis=-1, keepdims=True)
        e = jnp.exp(s - m)
        den = jnp.sum(e, axis=-1, keepdims=True)
        probs.append((e / den).astype(BF16))
        lses.append(m + jnp.log(den))
    for b, (r, j) in enumerate(blocks):
        vwin = v_ref[r, pl.ds(windows[b], kw), :]
        o = jnp.zeros((tq, HEAD_BLOCK), F32)
        lse = jnp.zeros((tq, HEAD_BLOCK), F32)
        for h in heads:
            sel = lane_head == h
            o = jnp.where(sel, jnp.dot(probs[b * HEADS_PER_BLOCK + h], vwin, preferred_element_type=F32), o)
            lse = jnp.where(sel, lses[b * HEADS_PER_BLOCK + h], lse)
        o_ref[r, j * tq:(j + 1) * tq, :] = o.astype(BF16)
        lse_ref[r, j * tq:(j + 1) * tq, :] = lse


def _dilated_attention(qkv, group):
    batch, dil, sub_len, _ = qkv.shape
    tq = min(128, sub_len)
    kw = min(256, sub_len)
    rows = min(DILATED_ROWS_PER_STEP, sub_len)
    n_res = min(dil, DILATED_ROWS_PER_STEP // rows)
    qmap = lambda b, r, i: (b, r, i, 0)
    return pl.pallas_call(
        functools.partial(_dilated_body, tq=tq, kw=kw, sub_len=sub_len),
        out_shape=(jax.ShapeDtypeStruct((batch, dil, sub_len, HEAD_BLOCK), BF16),
                   jax.ShapeDtypeStruct((batch, dil, sub_len, HEAD_BLOCK), F32)),
        grid=(batch, dil // n_res, sub_len // rows),
        in_specs=[
            pl.BlockSpec((None, n_res, rows, HEAD_BLOCK), qmap),
            pl.BlockSpec((None, n_res, sub_len, HEAD_BLOCK), lambda b, r, i: (b, r, 0, 1)),
            pl.BlockSpec((None, n_res, sub_len, HEAD_BLOCK), lambda b, r, i: (b, r, 0, 2)),
        ],
        out_specs=(pl.BlockSpec((None, n_res, rows, HEAD_BLOCK), qmap),
                   pl.BlockSpec((None, n_res, rows, HEAD_BLOCK), qmap)),
        compiler_params=_cparams(("arbitrary", "arbitrary", "arbitrary")),
        name=f"dilated_attn_g{group}",
    )(qkv, qkv, qkv)


def _ssm_body(uf_ref, ub_ref, bmat_ref, a_ref, cmat_ref, yf_ref, yb_ref, carry_ref, *state_refs, tm, pitch):
    n_slabs = SSM_N // LANES
    nb = uf_ref.shape[0]
    chains = [(d, b) for d in range(2) for b in range(nb)]

    @pl.when(pl.program_id(1) == 0)
    def _():
        carry_ref[...] = jnp.zeros_like(carry_ref)

    for c, (d, b) in enumerate(chains):
        u = (uf_ref if d == 0 else ub_ref)[b].astype(BF16)
        bu = jnp.dot(u, bmat_ref[d], preferred_element_type=F32)
        for part in range(2):
            for j in range(n_slabs):
                col = part * SSM_N + j * LANES
                state_refs[2 * c + part][j * pitch:j * pitch + tm, :] = bu[:, col:col + LANES]

    coef = [(a_ref[d, 0], a_ref[d, 1]) for d in range(2)]

    def step(t, hs):
        out = []
        for c, (d, b) in enumerate(chains):
            rows = pl.ds(t if d == 0 else tm - 1 - t, n_slabs, stride=pitch)
            ar, ai = coef[d]
            hr, hi = hs[2 * c], hs[2 * c + 1]
            nr = ar * hr - ai * hi + state_refs[2 * c][rows, :]
            ni = ar * hi + ai * hr + state_refs[2 * c + 1][rows, :]
            state_refs[2 * c][rows, :] = nr
            state_refs[2 * c + 1][rows, :] = ni
            out += [nr, ni]
        return tuple(out)

    final = lax.fori_loop(0, tm, step, tuple(carry_ref[k] for k in range(2 * len(chains))), unroll=8)
    for k, v in enumerate(final):
        carry_ref[k] = v

    for c, (d, b) in enumerate(chains):
        y = jnp.zeros((tm, B_WIDTH), F32)
        for part in range(2):
            for j in range(n_slabs):
                h = state_refs[2 * c + part][j * pitch:j * pitch + tm, :].astype(BF16)
                row0 = part * SSM_N + j * LANES
                y = y + jnp.dot(h, cmat_ref[d, row0:row0 + LANES, :], preferred_element_type=F32)
        (yf_ref if d == 0 else yb_ref)[b] = y


def _ssm_scan(u, bmat, coef, cmat, batch, seq):
    tm = SSM_TIME_TILE
    pitch = tm + 8
    nb = 2 if batch % 2 == 0 else 1
    n_t = seq // tm
    n_chains = 2 * nb
    fwd = lambda b, i: (b, i, 0)
    bwd = lambda b, i: (b, n_t - 1 - i, 0)
    const = lambda b, i: (0,) * 3
    u3 = u.reshape(batch, seq, B_WIDTH)
    yf, yb = pl.pallas_call(
        functools.partial(_ssm_body, tm=tm, pitch=pitch),
        out_shape=(jax.ShapeDtypeStruct((batch, seq, B_WIDTH), F32),) * 2,
        grid=(batch // nb, n_t),
        in_specs=[
            pl.BlockSpec((nb, tm, B_WIDTH), fwd),
            pl.BlockSpec((nb, tm, B_WIDTH), bwd),
            pl.BlockSpec(bmat.shape, const),
            pl.BlockSpec(coef.shape, lambda b, i: (0,) * 4),
            pl.BlockSpec(cmat.shape, const),
        ],
        out_specs=(pl.BlockSpec((nb, tm, B_WIDTH), fwd), pl.BlockSpec((nb, tm, B_WIDTH), bwd)),
        scratch_shapes=[pltpu.VMEM((2 * n_chains, 8, LANES), F32)]
        + [pltpu.VMEM((SSM_N // LANES * pitch, LANES), F32) for _ in range(2 * n_chains)],
        compiler_params=_cparams(("arbitrary", "arbitrary")),
        name="ssm_scan",
    )(u3, u3, bmat, coef, cmat)
    return yf.reshape(batch * seq, B_WIDTH), yb.reshape(batch * seq, B_WIDTH)


def _ssm_tables(lam_re, lam_im, log_dt, b_re, b_im, c_re, c_im):
    lr, li = lam_re.astype(F32), lam_im.astype(F32)
    dt = jnp.exp(log_dt.astype(F32))[:, None]
    mag = jnp.exp(lr * dt)
    a_r = mag * jnp.cos(li * dt)
    a_i = mag * jnp.sin(li * dt)
    den = lr * lr + li * li
    f_r = ((a_r - 1.0) * lr + a_i * li) / den
    f_i = (a_i * lr - (a_r - 1.0) * li) / den
    br, bi = b_re.astype(F32), b_im.astype(F32)
    bb_r = f_r[..., None] * br - f_i[..., None] * bi
    bb_i = f_r[..., None] * bi + f_i[..., None] * br
    eye = jnp.eye(B_GROUPS, dtype=F32)

    def in_mat(bb):
        return jnp.einsum('gpc,gh->gchp', bb, eye).reshape(B_WIDTH, SSM_N)

    def out_mat(c):
        return jnp.einsum('gcp,gh->gphc', c.astype(F32), eye).reshape(SSM_N, B_WIDTH)

    bmat = jnp.concatenate([in_mat(bb_r), in_mat(bb_i)], axis=1).astype(BF16)
    cmat = jnp.concatenate([out_mat(c_re), -out_mat(c_im)], axis=0).astype(BF16)

    coef = jnp.stack([a_r.reshape(SSM_N // LANES, LANES), a_i.reshape(SSM_N // LANES, LANES)])
    return bmat, coef, cmat


def _gelu_tanh(x):
    return 0.5 * x * (1.0 + jnp.tanh(math.sqrt(2.0 / math.pi) * (x + 0.044715 * (x * x * x))))


def _token_order(ref, unperm_ref, split):
    tm = ref.shape[0] * ref.shape[1]
    v = ref[...].reshape(tm, ref.shape[2])
    if unperm_ref is None:
        return v.astype(F32)
    if not split:
        return jnp.dot(unperm_ref[...], v, preferred_element_type=F32)
    hi = v.astype(BF16)
    lo = (v - hi.astype(F32)).astype(BF16)
    return (jnp.dot(unperm_ref[...], hi, preferred_element_type=F32)
            + jnp.dot(unperm_ref[...], lo, preferred_element_type=F32))


def _even_out_body(x_ref, o0_ref, o1_ref, o2_ref, l0_ref, l1_ref, l2_ref, u_ref, yf_ref, yb_ref,
                   d_ref, wglu_ref, wout_ref, unperm1_ref, unperm2_ref, out_ref):
    unperms = (None, unperm1_ref, unperm2_ref)
    l0, l1, l2 = (_token_order(r, p, True) for r, p in zip((l0_ref, l1_ref, l2_ref), unperms))
    o0, o1, o2 = (_token_order(r, p, False) for r, p in zip((o0_ref, o1_ref, o2_ref), unperms))
    lmax = jnp.maximum(jnp.maximum(l0, l1), l2)
    e0, e1, e2 = jnp.exp(l0 - lmax), jnp.exp(l1 - lmax), jnp.exp(l2 - lmax)
    den = e0 + e1 + e2
    a = (e0 / den) * o0 + (e1 / den) * o1 + (e2 / den) * o2
    y = u_ref[...] * d_ref[...] + yf_ref[...] + yb_ref[...]
    z = _gelu_tanh(y)
    z = z * jax.nn.sigmoid(jnp.dot(z.astype(BF16), wglu_ref[...], preferred_element_type=F32))
    acc = jnp.dot(a.astype(BF16), wout_ref[0:HEAD_BLOCK, :], preferred_element_type=F32)
    acc = acc + jnp.dot(z.astype(BF16), wout_ref[HEAD_BLOCK:, :], preferred_element_type=F32)
    out_ref[...] = x_ref[...] + acc


def _even_out(x, os_, ls_, u, yf, yb, d_skip, wglu, wout, seq, tm):
    n = x.shape[0]
    n_seq_tiles = seq // tm
    row = lambda i: (i, 0)
    const = lambda i: (0, 0)
    narrow = pl.BlockSpec((tm, HEAD_BLOCK), row)
    grouped = [pl.BlockSpec((None, dil, tm // dil, HEAD_BLOCK), lambda i: (i // n_seq_tiles, 0, i % n_seq_tiles, 0))
               for _, dil in A_GROUPS]
    unperms = [_residue_permutation(tm, dil).T for _, dil in A_GROUPS[1:]]
    return pl.pallas_call(
        _even_out_body,
        out_shape=jax.ShapeDtypeStruct((n, D_MODEL), F32),
        grid=(n // tm,),
        in_specs=[pl.BlockSpec((tm, D_MODEL), row)] + grouped + grouped + [narrow] * 3 + [
            pl.BlockSpec((1, B_WIDTH), const),
            pl.BlockSpec(wglu.shape, const),
            pl.BlockSpec(wout.shape, const),
            pl.BlockSpec((tm, tm), const),
            pl.BlockSpec((tm, tm), const),
        ],
        out_specs=pl.BlockSpec((tm, D_MODEL), row),
        compiler_params=_cparams(("arbitrary",)),
        name="even_out",
    )(x, *os_, *ls_, u, yf, yb, d_skip, wglu, wout, *unperms)


def _ffn_body(x_ref, g_ref, wg_ref, wu_ref, wd_ref, emb_ref, pg_ref, pwp_ref, pwg_ref, out_ref, *, chunk):
    x = x_ref[...]
    xn = _rms(x, g_ref[...]).astype(BF16)
    n_chunks = D_FF // chunk

    def gate_up(c):
        cols = slice(c * chunk, (c + 1) * chunk)
        return (jnp.dot(xn, wg_ref[:, cols], preferred_element_type=F32),
                jnp.dot(xn, wu_ref[:, cols], preferred_element_type=F32))

    acc = jnp.zeros(x.shape, F32)
    nxt = gate_up(0)
    for c in range(n_chunks):
        g, up = nxt
        if c + 1 < n_chunks:
            nxt = gate_up(c + 1)
        h = (g * jax.nn.sigmoid(g) * up).astype(BF16)
        acc = acc + jnp.dot(h, wd_ref[c * chunk:(c + 1) * chunk, :], preferred_element_type=F32)
    x = x + acc
    xn = _rms(x, pg_ref[...]).astype(BF16)
    gate = jax.nn.sigmoid(jnp.dot(xn, pwg_ref[...], preferred_element_type=F32))
    emb = jnp.dot(emb_ref[...].astype(BF16), pwp_ref[...], preferred_element_type=F32)
    out_ref[...] = x + emb * gate


def _ffn_ple(x, ln, wg, wu, wd, p, ple_ln, ple_wp, ple_wg, tm):
    n = x.shape[0]
    row = lambda i: (i, 0)
    const = lambda i: (0, 0)
    return pl.pallas_call(
        functools.partial(_ffn_body, chunk=256),
        out_shape=jax.ShapeDtypeStruct((n, D_MODEL), F32),
        grid=(n // tm,),
        in_specs=[
            pl.BlockSpec((tm, D_MODEL), row),
            pl.BlockSpec((1, D_MODEL), const),
            pl.BlockSpec(wg.shape, const),
            pl.BlockSpec(wu.shape, const),
            pl.BlockSpec(wd.shape, const),
            pl.BlockSpec((tm, PLE_DIM), row),
            pl.BlockSpec((1, D_MODEL), const),
            pl.BlockSpec(ple_wp.shape, const),
            pl.BlockSpec(ple_wg.shape, const),
        ],
        out_specs=pl.BlockSpec((tm, D_MODEL), row),
        compiler_params=_cparams(("arbitrary",)),
        name="ffn_dense_embed",
    )(x, ln, wg, wu, wd, p, ple_ln, ple_wp, ple_wg)


def _gqa_body(qt_ref, k_ref, vt_ref, o_ref, qs_ref, s_ref, smax_ref, m_ref, acc_ref, *, tq, tk, seq):
    n_kv = seq // tk
    vt_tile = vt_ref.shape[2]
    for c in range(C_Q_WIDTH // HEAD_BLOCK):
        qs_ref[...] = jnp.zeros(qs_ref.shape, BF16)
        for g in range(HEADS_PER_BLOCK):
            rows = slice(c * HEAD_BLOCK + g * HEAD_DIM, c * HEAD_BLOCK + (g + 1) * HEAD_DIM)
            qs_ref[g * HEAD_DIM:(g + 1) * HEAD_DIM, g * tq:(g + 1) * tq] = qt_ref[rows, :]
        m_ref[...] = jnp.full(m_ref.shape, NEG_INF, F32)
        acc_ref[...] = jnp.zeros(acc_ref.shape, F32)

        def scores(t, slot):
            k0 = pl.multiple_of(t * tk, tk)
            s = jnp.dot(k_ref[pl.ds(k0, tk), :], qs_ref[...], preferred_element_type=F32)
            s_ref[slot] = s
            smax_ref[slot] = jnp.max(s, axis=0, keepdims=True)

        def consume(t, slot):
            m_old = m_ref[...]
            m_new = jnp.maximum(m_old, smax_ref[slot])
            alpha = jnp.exp2(m_old - m_new)
            m_ref[...] = m_new
            for g in range(HEADS_PER_BLOCK):
                cols = slice(g * tq, (g + 1) * tq)
                p = jnp.exp2(s_ref[slot, :, cols] - m_new[:, cols]).astype(BF16)
                pv = alpha[:, cols] * acc_ref[g]
                for part in range(tk // vt_tile):
                    vt = vt_ref[t * (tk // vt_tile) + part, g * V_HEAD_ROWS:(g + 1) * V_HEAD_ROWS, :]
                    pv = pv + jnp.dot(vt, p[part * vt_tile:(part + 1) * vt_tile, :], preferred_element_type=F32)
                acc_ref[g] = pv

        scores(0, 0)

        def trip(u, last):
            scores(2 * u + 1, 1)
            consume(2 * u, 0)
            if not last:
                scores(2 * u + 2, 0)
            consume(2 * u + 1, 1)

        def step(u, carry):
            trip(u, False)
            return carry

        if n_kv > 2:
            lax.fori_loop(0, n_kv // 2 - 1, step, 0)
        trip(n_kv // 2 - 1, True)
        heads = [acc_ref[g, 0:HEAD_DIM, :] / acc_ref[g, HEAD_DIM:HEAD_DIM + 1, :] for g in range(HEADS_PER_BLOCK)]
        o_ref[:, c * HEAD_BLOCK:(c + 1) * HEAD_BLOCK] = jnp.concatenate(heads, axis=0).T.astype(BF16)


def _gqa_attention(qt, k, vt, batch, seq, tq, tk, tm):
    assert tm % tq == 0 and tk % tm == 0 and seq % (2 * tk) == 0
    per_tm = tm // tq
    tiles_per_seq = seq // tm
    return pl.pallas_call(
        functools.partial(_gqa_body, tq=tq, tk=tk, seq=seq),
        out_shape=jax.ShapeDtypeStruct((batch, seq, C_Q_WIDTH), BF16),
        grid=(batch, seq // tq),
        in_specs=[
            pl.BlockSpec((None, C_Q_WIDTH, tq), lambda b, i: (b * tiles_per_seq + i // per_tm, 0, i % per_tm)),
            pl.BlockSpec((None, seq, C_KV_WIDTH), lambda b, i: (b, 0, 0)),
            pl.BlockSpec((tiles_per_seq, HEADS_PER_BLOCK * V_HEAD_ROWS, tm), lambda b, i: (b, 0, 0)),
        ],
        out_specs=pl.BlockSpec((None, tq, C_Q_WIDTH), lambda b, i: (b, i, 0)),
        scratch_shapes=[pltpu.VMEM((HEAD_BLOCK, HEADS_PER_BLOCK * tq), BF16),
                        pltpu.VMEM((2, tk, HEADS_PER_BLOCK * tq), F32),
                        pltpu.VMEM((2, 1, HEADS_PER_BLOCK * tq), F32),
                        pltpu.VMEM((1, HEADS_PER_BLOCK * tq), F32),
                        pltpu.VMEM((HEADS_PER_BLOCK, V_HEAD_ROWS, tq), F32)],
        compiler_params=_cparams(("arbitrary", "arbitrary")),
        name="gqa_attention",
    )(qt, k.reshape(batch, seq, C_KV_WIDTH), vt).reshape(batch * seq, C_Q_WIDTH)


def _odd_out_body(x_ref, a_ref, wout_ref, g_ref, rw_hi_ref, rw_lo_ref, out_ref, xn_ref, route_ref, route_t_ref):
    x = x_ref[...] + jnp.dot(a_ref[...], wout_ref[...], preferred_element_type=F32)
    out_ref[...] = x
    h = _rms(x, g_ref[...])
    h_hi = h.astype(BF16)
    h_lo = (h - h_hi.astype(F32)).astype(BF16)
    logits = (jnp.dot(h_hi, rw_hi_ref[...], preferred_element_type=F32)
              + jnp.dot(h_hi, rw_lo_ref[...], preferred_element_type=F32)
              + jnp.dot(h_lo, rw_hi_ref[...], preferred_element_type=F32))
    lane = lax.broadcasted_iota(jnp.int32, logits.shape, 1)
    logits = jnp.where(lane < N_EXPERTS, logits, -jnp.inf)
    m1 = jnp.max(logits, axis=-1, keepdims=True)
    i1 = jnp.min(jnp.where(logits == m1, lane, LANES), axis=-1, keepdims=True)
    rest = jnp.where(lane == i1, -jnp.inf, logits)
    m2 = jnp.max(rest, axis=-1, keepdims=True)
    i2 = jnp.min(jnp.where(rest == m2, lane, LANES), axis=-1, keepdims=True)
    e2 = jnp.exp(m2 - m1)
    w1 = 1.0 / (1.0 + e2)
    w2 = e2 / (1.0 + e2)
    rec = jnp.where(lane == 0, i1.astype(F32),
                    jnp.where(lane == 1, i2.astype(F32), jnp.where(lane == 2, w1, jnp.where(lane == 3, w2, 0.0))))
    route_ref[...] = rec
    route_t_ref[...] = rec.T[0:8, :]
    xn_ref[...] = _pack_pairs(h_hi[:, :D_MODEL // 2].astype(F32), h_hi[:, D_MODEL // 2:].astype(F32))


def _odd_out(x, attn, wout, ln, rw_hi, rw_lo, tm):
    n = x.shape[0]
    row = lambda i: (i, 0)
    const = lambda i: (0, 0)
    return pl.pallas_call(
        _odd_out_body,
        out_shape=(jax.ShapeDtypeStruct((n, D_MODEL), F32), jax.ShapeDtypeStruct((n, D_MODEL // 2), jnp.uint32),
                   jax.ShapeDtypeStruct((n, LANES), F32), jax.ShapeDtypeStruct((8, n), F32)),
        grid=(n // tm,),
        in_specs=[
            pl.BlockSpec((tm, D_MODEL), row),
            pl.BlockSpec((tm, C_Q_WIDTH), row),
            pl.BlockSpec(wout.shape, const),
            pl.BlockSpec((1, D_MODEL), const),
            pl.BlockSpec(rw_hi.shape, const),
            pl.BlockSpec(rw_lo.shape, const),
        ],
        out_specs=(pl.BlockSpec((tm, D_MODEL), row), pl.BlockSpec((tm, D_MODEL // 2), row),
                   pl.BlockSpec((tm, LANES), row), pl.BlockSpec((8, tm), lambda i: (0, i))),
        compiler_params=_cparams(("arbitrary",)),
        name="odd_out_router",
    )(x, attn, wout, ln, rw_hi, rw_lo)


def _pack_pairs(lo, hi):
    lo_bits = lax.bitcast_convert_type(lo.astype(BF16).astype(F32), jnp.uint32) >> 16
    hi_bits = lax.bitcast_convert_type(hi.astype(BF16).astype(F32), jnp.uint32) & jnp.uint32(0xFFFF0000)
    return lo_bits | hi_bits


def _unpack_pairs(u):
    lo = lax.bitcast_convert_type(u << 16, F32)
    hi = lax.bitcast_convert_type(u & jnp.uint32(0xFFFF0000), F32)
    return lo, hi


def _rank_body(route_ref, tri_ref, pos_ref, meta_ref, cnt_ref, off_ref, *, tile_rows):
    phase = pl.program_id(0)
    i = pl.program_id(1)
    tm = route_ref.shape[1]
    route = route_ref[...]
    i1 = route[0:1, :]
    i2 = route[1:2, :]
    row = lax.broadcasted_iota(jnp.int32, (N_EXPERTS, tm), 0).astype(F32)
    hit = ((row == i1) | (row == i2)).astype(F32)
    per_expert = jnp.sum(hit, axis=1, keepdims=True)
    erow = lax.broadcasted_iota(jnp.int32, (N_EXPERTS, 1), 0)

    @pl.when((phase == 0) & (i == 0))
    def _():
        cnt_ref[...] = jnp.zeros_like(cnt_ref)

    @pl.when(phase == 0)
    def _():
        cnt_ref[...] += per_expert

    @pl.when((phase == 1) & (i == 0))
    def _():
        tot = cnt_ref[...]
        padded = jnp.floor((tot + (tile_rows - 1)) * (1.0 / tile_rows)) * tile_rows
        off = jnp.zeros_like(tot)
        for e in range(N_EXPERTS - 1):
            pe = jnp.sum(jnp.where(erow == e, padded, 0.0), axis=0, keepdims=True)
            off = off + jnp.where(erow > e, pe, 0.0)
        off_ref[...] = off
        lane = lax.broadcasted_iota(jnp.int32, meta_ref.shape, 1)
        meta_ref[...] = jnp.where(lane == 0, off, jnp.where(lane == 1, tot, 0.0))
        cnt_ref[...] = jnp.zeros_like(cnt_ref)

    @pl.when(phase == 1)
    def _():
        hit16 = jnp.concatenate([hit, jnp.zeros_like(hit)], axis=0).astype(BF16)
        before = jnp.dot(hit16, tri_ref[...], preferred_element_type=F32)[0:N_EXPERTS, :]
        slot_row = off_ref[...] + cnt_ref[...] + before
        pos1 = jnp.sum(jnp.where(row == i1, slot_row, 0.0), axis=0, keepdims=True)
        pos2 = jnp.sum(jnp.where(row == i2, slot_row, 0.0), axis=0, keepdims=True)
        pos_ref[...] = jnp.concatenate([pos1, pos2, jnp.zeros((6, tm), F32)], axis=0).astype(jnp.int32)
        cnt_ref[...] += per_expert


def _rank(route_t, tri, tm, tile_rows):
    n = route_t.shape[1]
    return pl.pallas_call(
        functools.partial(_rank_body, tile_rows=tile_rows),
        out_shape=(jax.ShapeDtypeStruct((n // tm, 8, tm), jnp.int32), jax.ShapeDtypeStruct((N_EXPERTS, LANES), F32)),
        grid=(2, n // tm),
        in_specs=[pl.BlockSpec((8, tm), lambda ph, i: (0, i)), pl.BlockSpec(tri.shape, lambda ph, i: (0, 0))],
        out_specs=(pl.BlockSpec((None, 8, tm), lambda ph, i: (i * ph, 0, 0)),
                   pl.BlockSpec((N_EXPERTS, LANES), lambda ph, i: (0, 0))),
        scratch_shapes=[pltpu.VMEM((N_EXPERTS, 1), F32), pltpu.VMEM((N_EXPERTS, 1), F32)],
        compiler_params=_cparams(("arbitrary", "arbitrary")),
        name="moe_rank",
    )(route_t, tri)


def _row_copy(src, src_row, dst, dst_row, sem):
    return pltpu.make_async_copy(src.at[pl.ds(src_row, 1)], dst.at[pl.ds(dst_row, 1)], sem)


def _dispatch_body(pos_ref, xn_ref, buf_ref, xs_ref, sem, *, tm):
    del buf_ref

    def issue(t, carry):
        _row_copy(xn_ref, t, xs_ref, pos_ref[0, t], sem).start()
        _row_copy(xn_ref, t, xs_ref, pos_ref[1, t], sem).start()
        return carry

    lax.fori_loop(0, tm, issue, 0, unroll=8)
    for _ in range(TOP_K):
        pltpu.make_async_copy(xn_ref, xs_ref.at[pl.ds(0, tm)], sem).wait()


def _dispatch(pos, xn, rows, tm):
    n = xn.shape[0]
    buf = jnp.zeros((rows, xn.shape[1]), xn.dtype)
    return pl.pallas_call(
        functools.partial(_dispatch_body, tm=tm),
        out_shape=jax.ShapeDtypeStruct(buf.shape, buf.dtype),
        grid=(n // tm,),
        in_specs=[pl.BlockSpec((None, 8, tm), lambda i: (i, 0, 0), memory_space=pltpu.SMEM),
                  pl.BlockSpec((tm, xn.shape[1]), lambda i: (i, 0)), pl.BlockSpec(memory_space=pl.ANY)],
        out_specs=pl.BlockSpec(memory_space=pl.ANY),
        scratch_shapes=[pltpu.SemaphoreType.DMA(())],
        input_output_aliases={2: 0},
        compiler_params=_cparams(("arbitrary",)),
        name="moe_dispatch",
    )(pos, xn, buf)


def _expert_ffn_body(te_ref, tv_ref, xs_ref, wg_ref, wu_ref, wd_ref, ys_ref, *, chunk):
    del te_ref
    half = D_MODEL // 2
    n_chunks = D_EXPERT // chunk
    valid = tv_ref[pl.program_id(0)] != 0

    @pl.when(valid)
    def _():
        lo, hi = _unpack_pairs(xs_ref[...])
        xa, xb = lo.astype(BF16), hi.astype(BF16)

        def gate_up(c):
            cols = slice(c * chunk, (c + 1) * chunk)
            g = (jnp.dot(xa, wg_ref[0:half, cols], preferred_element_type=F32)
                 + jnp.dot(xb, wg_ref[half:, cols], preferred_element_type=F32))
            up = (jnp.dot(xa, wu_ref[0:half, cols], preferred_element_type=F32)
                  + jnp.dot(xb, wu_ref[half:, cols], preferred_element_type=F32))
            return g, up

        acc = jnp.zeros((xs_ref.shape[0], D_MODEL), F32)
        nxt = gate_up(0)
        for c in range(n_chunks):
            g, up = nxt
            if c + 1 < n_chunks:
                nxt = gate_up(c + 1)
            h = (g * jax.nn.sigmoid(g) * up).astype(BF16)
            acc = acc + jnp.dot(h, wd_ref[c * chunk:(c + 1) * chunk, :], preferred_element_type=F32)
        ys_ref[...] = _pack_pairs(acc[:, :half], acc[:, half:])

    @pl.when(jnp.logical_not(valid))
    def _():
        ys_ref[...] = jnp.zeros_like(ys_ref)


def _expert_ffn(tile_expert, tile_valid, xs, wg, wu, wd, tile_rows, chunk):
    rows, half = xs.shape
    resident = dict(pipeline_mode=pl.Buffered(1))
    grid_spec = pltpu.PrefetchScalarGridSpec(
        num_scalar_prefetch=2,
        grid=(rows // tile_rows,),
        in_specs=[
            pl.BlockSpec((tile_rows, half), lambda j, te, tv: (j, 0)),
            pl.BlockSpec((None, D_MODEL, D_EXPERT), lambda j, te, tv: (te[j], 0, 0), **resident),
            pl.BlockSpec((None, D_MODEL, D_EXPERT), lambda j, te, tv: (te[j], 0, 0), **resident),
            pl.BlockSpec((None, D_EXPERT, D_MODEL), lambda j, te, tv: (te[j], 0, 0), **resident),
        ],
        out_specs=pl.BlockSpec((tile_rows, half), lambda j, te, tv: (j, 0)),
    )
    return pl.pallas_call(
        functools.partial(_expert_ffn_body, chunk=chunk),
        out_shape=jax.ShapeDtypeStruct(xs.shape, xs.dtype),
        grid_spec=grid_spec,
        compiler_params=_cparams(("arbitrary",)),
        name="moe_expert_ffn",
    )(tile_expert, tile_valid, xs, wg, wu, wd)


def _moe_routed(xn, route_t, wg, wu, wd):
    n = xn.shape[0]
    tm = MOE_TOKEN_TILE
    tile_rows = MOE_ROW_TILE
    rows = TOP_K * n + N_EXPERTS * tile_rows
    tri = jnp.asarray(np.triu(np.ones((tm, tm), np.float32), k=1), dtype=BF16)
    pos, meta = _rank(route_t, tri, tm, tile_rows)
    seg_end = meta[:, 0] + jnp.ceil(meta[:, 1] / tile_rows) * tile_rows
    tile_start = (jnp.arange(rows // tile_rows) * tile_rows).astype(F32)
    tile_expert = jnp.minimum(jnp.sum(tile_start[:, None] >= seg_end[None, :], axis=1), N_EXPERTS - 1).astype(jnp.int32)
    tile_valid = (tile_start < seg_end[N_EXPERTS - 1]).astype(jnp.int32)
    xs = _dispatch(pos, xn, rows, tm)
    return pos, _expert_ffn(tile_expert, tile_valid, xs, wg, wu, wd, tile_rows, MOE_FF_CHUNK)


def _moe_ple_body(pos_ref, ys_ref, x_ref, route_ref, p_ref, g_ref, wp_ref, wg_ref, out_ref, y1_ref, y2_ref, sem):
    tm = x_ref.shape[0]

    def issue(t, carry):
        _row_copy(ys_ref, pos_ref[0, t], y1_ref, t, sem).start()
        _row_copy(ys_ref, pos_ref[1, t], y2_ref, t, sem).start()
        return carry

    lax.fori_loop(0, tm, issue, 0, unroll=8)
    pltpu.make_async_copy(ys_ref.at[pl.ds(0, tm)], y1_ref, sem).wait()
    pltpu.make_async_copy(ys_ref.at[pl.ds(0, tm)], y2_ref, sem).wait()
    route = route_ref[...]
    w1 = route[:, 2:3]
    w2 = route[:, 3:4]
    lo1, hi1 = _unpack_pairs(y1_ref[...])
    lo2, hi2 = _unpack_pairs(y2_ref[...])
    moe = jnp.concatenate([w1 * lo1 + w2 * lo2, w1 * hi1 + w2 * hi2], axis=1)
    x = x_ref[...] + moe
    xn = _rms(x, g_ref[...]).astype(BF16)
    gate = jax.nn.sigmoid(jnp.dot(xn, wg_ref[...], preferred_element_type=F32))
    emb = jnp.dot(p_ref[...].astype(BF16), wp_ref[...], preferred_element_type=F32)
    out_ref[...] = x + emb * gate


def _moe_ple(pos, ys, x, route, p, ln, wp, wg):
    n = x.shape[0]
    tm = pos.shape[2]
    row = lambda i: (i, 0)
    const = lambda i: (0, 0)
    return pl.pallas_call(
        _moe_ple_body,
        out_shape=jax.ShapeDtypeStruct((n, D_MODEL), F32),
        grid=(n // tm,),
        in_specs=[
            pl.BlockSpec((None, 8, tm), lambda i: (i, 0, 0), memory_space=pltpu.SMEM),
            pl.BlockSpec(memory_space=pl.ANY),
            pl.BlockSpec((tm, D_MODEL), row),
            pl.BlockSpec((tm, LANES), row),
            pl.BlockSpec((tm, PLE_DIM), row),
            pl.BlockSpec((1, D_MODEL), const),
            pl.BlockSpec(wp.shape, const),
            pl.BlockSpec(wg.shape, const),
        ],
        out_specs=pl.BlockSpec((tm, D_MODEL), row),
        scratch_shapes=[pltpu.VMEM((tm, D_MODEL // 2), jnp.uint32), pltpu.VMEM((tm, D_MODEL // 2), jnp.uint32),
                        pltpu.SemaphoreType.DMA(())],
        compiler_params=_cparams(("arbitrary",)),
        name="moe_combine_embed",
    )(pos, ys, x, route, p, ln, wp, wg)


def _rope_tables(pos, dim):
    inv = ROPE_THETA ** (-jnp.arange(0, dim, 2, dtype=F32) / dim)
    ang = pos.astype(F32)[:, None] * inv[None, :]
    ang = jnp.concatenate([ang, ang], axis=-1)
    return jnp.cos(ang), jnp.sin(ang)


def _tile_heads(t):
    return jnp.tile(t, (1, HEADS_PER_BLOCK))


def _even_rope(seq):
    cos, sin = _rope_tables(jnp.arange(seq), HEAD_DIM)
    sign = jnp.where(jnp.arange(HEAD_DIM) < HEAD_DIM // 2, -1.0, 1.0).astype(F32)
    return _tile_heads(cos), _tile_heads(sin * sign)


def _odd_rope(seq):
    t = jnp.arange(seq)
    half = HEAD_DIM // 2
    cr, sr = _rope_tables(t // GRID_W, half)
    cc, sc = _rope_tables(t % GRID_W, half)
    sign = jnp.where(jnp.arange(half) < half // 2, -1.0, 1.0).astype(F32)
    cos = jnp.concatenate([cr, cc], axis=-1)
    sin = jnp.concatenate([sr * sign, sc * sign], axis=-1)
    return _tile_heads(cos), _tile_heads(sin)


def _head_mean_matrix():
    blk = np.arange(HEAD_BLOCK) // HEAD_DIM
    return jnp.asarray((blk[:, None] == blk[None, :]).astype(np.float32) / HEAD_DIM, dtype=BF16)


def _q_head_permutation():
    n_blocks = C_Q_WIDTH // HEAD_BLOCK
    heads = [HEADS_PER_BLOCK * g + c for c in range(n_blocks) for g in range(HEADS_PER_BLOCK)]
    return np.concatenate([np.arange(h * HEAD_DIM, (h + 1) * HEAD_DIM) for h in heads])


def _row(v):
    return v.astype(F32).reshape(1, -1)


def _even_layer(x, p, batch, seq, w, ple):
    tm = 512
    cos, sin = _even_rope(seq)
    *qkvs, u = _proj_even(x, w['ln_mix'], w['w_in'], w['bd'], w['qn'], w['kn'], cos, sin, batch, seq, tm)
    outs, lses = [], []
    for g, qkv in enumerate(qkvs):
        o, lse = _dilated_attention(qkv, g)
        outs.append(o)
        lses.append(lse)
    tabs = [_ssm_tables(*(w[k][direction] for k in ('lam_re', 'lam_im', 'log_dt', 'b_re', 'b_im', 'c_re', 'c_im')))
            for direction in range(2)]
    ys = _ssm_scan(u, *(jnp.stack(t) for t in zip(*tabs)), batch, seq)
    x = _even_out(x, outs, lses, u, ys[0], ys[1], w['d_skip'], w['w_glu'], w['w_out'], seq, tm)
    return _ffn_ple(x, w['ln_ffn'], w['ffn_gate'], w['ffn_up'], w['ffn_down'], p, *ple, tm)


def _odd_layer(x, p, batch, seq, w, ple):
    tm = 512
    cos, sin = _odd_rope(seq)
    qt, k, vt = _proj_odd(x, w['ln_mix'], w['w_in'], w['bd'], w['qn'], w['kn'], cos, sin, seq, tm)
    attn = _gqa_attention(qt, k, vt, batch, seq, 256, GQA_KV_TILE, tm)
    x, xn, route, route_t = _odd_out(x, attn, w['w_out'], w['ln_ffn'], w['router_hi'], w['router_lo'],
                                     MOE_TOKEN_TILE)
    pos, ys = _moe_routed(xn, route_t, w['moe_gate'], w['moe_up'], w['moe_down'])
    return _moe_ple(pos, ys, x, route, p, *ple)


def _prepare_even(j, ln_mix_e, w_in_e, a_qnorm, a_knorm, lam_re, lam_im, log_dt, b_re, b_im, c_re, c_im,
                  ssm_d, ssm_w_glu, w_out_e, ln_ffn_e, ffn_w_gate, ffn_w_up, ffn_w_down):
    return dict(
        ln_mix=_row(ln_mix_e[j]), w_in=w_in_e[j].astype(BF16), bd=_head_mean_matrix(),
        qn=_tile_heads(_row(a_qnorm[j])), kn=_tile_heads(_row(a_knorm[j])),
        lam_re=lam_re[j], lam_im=lam_im[j], log_dt=log_dt[j], b_re=b_re[j], b_im=b_im[j],
        c_re=c_re[j], c_im=c_im[j], d_skip=_row(ssm_d[j]), w_glu=ssm_w_glu[j].astype(BF16),
        w_out=w_out_e[j].astype(BF16), ln_ffn=_row(ln_ffn_e[j]), ffn_gate=ffn_w_gate[j].astype(BF16),
        ffn_up=ffn_w_up[j].astype(BF16), ffn_down=ffn_w_down[j].astype(BF16))


def _prepare_odd(j, ln_mix_o, w_in_o, c_qnorm, c_knorm, w_out_o, ln_ffn_o, router_w, moe_w_gate, moe_w_up,
                 moe_w_down):
    perm = _q_head_permutation()
    w_in = w_in_o[j]
    w_in = jnp.concatenate([w_in[:, :C_Q_WIDTH][:, perm], w_in[:, C_Q_WIDTH:]], axis=1).astype(BF16)
    rw = jnp.pad(router_w[j].astype(F32), ((0, 0), (0, LANES - N_EXPERTS)))
    rw_hi = rw.astype(BF16)
    rw_lo = (rw - rw_hi.astype(F32)).astype(BF16)
    return dict(
        ln_mix=_row(ln_mix_o[j]), w_in=w_in, bd=_head_mean_matrix(),
        qn=_tile_heads(_row(c_qnorm[j])), kn=_tile_heads(_row(c_knorm[j])),
        w_out=w_out_o[j][perm, :].astype(BF16), ln_ffn=_row(ln_ffn_o[j]), router_hi=rw_hi, router_lo=rw_lo,
        moe_gate=moe_w_gate[j].astype(BF16), moe_up=moe_w_up[j].astype(BF16), moe_down=moe_w_down[j].astype(BF16))


def _trunk(x, p, layers, ple):
    batch, seq, _ = x.shape
    x = x.reshape(batch * seq, D_MODEL)
    for i, w in enumerate(layers):
        pi = p[i].reshape(batch * seq, PLE_DIM)
        x = (_even_layer if i % 2 == 0 else _odd_layer)(x, pi, batch, seq, w, ple[i])
    return x.reshape(batch, seq, D_MODEL)


def kernel(x_prompt, x_sample, p_prompt, p_sample, ln_mix_e, w_in_e, a_qnorm, a_knorm, ssm_lam_re, ssm_lam_im, ssm_log_dt, ssm_b_re, ssm_b_im, ssm_c_re, ssm_c_im, ssm_d, ssm_w_glu, w_out_e, ln_ffn_e, ffn_w_gate, ffn_w_up, ffn_w_down, ln_mix_o, w_in_o, c_qnorm, c_knorm, w_out_o, ln_ffn_o, router_w, moe_w_gate, moe_w_up, moe_w_down, ple_ln, ple_w_proj, ple_w_gate):
    depth = p_prompt.shape[0]
    layers = []
    for i in range(depth):
        j = i // 2
        if i % 2 == 0:
            layers.append(_prepare_even(j, ln_mix_e, w_in_e, a_qnorm, a_knorm, ssm_lam_re, ssm_lam_im, ssm_log_dt,
                                        ssm_b_re, ssm_b_im, ssm_c_re, ssm_c_im, ssm_d, ssm_w_glu, w_out_e, ln_ffn_e,
                                        ffn_w_gate, ffn_w_up, ffn_w_down))
        else:
            layers.append(_prepare_odd(j, ln_mix_o, w_in_o, c_qnorm, c_knorm, w_out_o, ln_ffn_o, router_w,
                                       moe_w_gate, moe_w_up, moe_w_down))
    ple = [(_row(ple_ln[i]), ple_w_proj[i].astype(BF16), ple_w_gate[i].astype(BF16)) for i in range(depth)]
    y_prompt = _trunk(x_prompt, p_prompt, layers, ple)
    y_sample = _trunk(x_sample, p_sample, layers, ple)
    return (y_prompt, y_sample)
```
